```python
import jax, jax.numpy as jnp
from jax import lax
import numpy as np

D_MODEL = 1024
BATCH = 4
SEQ = 8192
DEPTH = 1

SB_HEADS = 8
SB_HEAD_DIM = 128
SB_WIDTH = SB_HEADS * SB_HEAD_DIM
GLA_HEADS = 4
GLA_KEY_DIM = 128
GLA_VAL_DIM = 256
GLA_KEY_WIDTH = GLA_HEADS * GLA_KEY_DIM
GLA_WIDTH = GLA_HEADS * GLA_VAL_DIM
D_INNER = SB_WIDTH + GLA_WIDTH
GATE_RANK = 16
GATE_TEMP = 16.0
SB_BLOCK = 128
GLA_CHUNK = 64
NORM_EPS = 1e-6
IN_SIZES = (SB_WIDTH, SB_WIDTH, SB_WIDTH, SB_WIDTH,
            GLA_KEY_WIDTH, GLA_KEY_WIDTH, GLA_WIDTH, GLA_WIDTH, GATE_RANK)
IN_PROJ_WIDTH = sum(IN_SIZES)

kernel_name = "hymba_stickbreak_gla_block"


def rmsnorm(x, w):
    xf = x.astype(jnp.float32)
    xf = xf * lax.rsqrt(jnp.mean(xf * xf, axis=-1, keepdims=True) + NORM_EPS)
    return (xf * w.astype(jnp.float32)).astype(x.dtype)


def head_rmsnorm(o, w, n_heads):
    B, S, W = o.shape
    oh = o.reshape(B, S, n_heads, W // n_heads)
    return rmsnorm(oh, w.reshape(n_heads, W // n_heads)).reshape(B, S, W)


def split_heads(t, n_heads):
    B, S, W = t.shape
    return t.reshape(B, S, n_heads, W // n_heads).transpose(0, 2, 1, 3)


def merge_heads(t):
    B, H, S, d = t.shape
    return t.transpose(0, 2, 1, 3).reshape(B, S, H * d)


def stick_breaking_attention(q, k, v):
    B, H, S, d = q.shape
    nb = S // SB_BLOCK
    qf = q.astype(jnp.float32) * (d ** -0.5)
    kf = k.astype(jnp.float32)
    vf = v.astype(jnp.float32)
    q_blocks = qf.reshape(B, H, nb, SB_BLOCK, d).transpose(2, 0, 1, 3, 4)
    pos = jnp.arange(SB_BLOCK)
    strict_lower = pos[None, :] < pos[:, None]
    all_valid = jnp.ones((SB_BLOCK, SB_BLOCK), dtype=bool)

    def one_query_block(args):
        i, qb = args

        def body(n, carry):
            acc, logsurv = carry
            j = i - n
            kb = lax.dynamic_slice_in_dim(kf, j * SB_BLOCK, SB_BLOCK, axis=2)
            vb = lax.dynamic_slice_in_dim(vf, j * SB_BLOCK, SB_BLOCK, axis=2)
            z = jnp.einsum('bhtd,bhsd->bhts', qb, kb)
            valid = jnp.where(j == i, strict_lower, all_valid)
            log_fail = jnp.where(valid, jax.nn.log_sigmoid(-z), 0.0)
            after = lax.cumsum(log_fail, axis=3, reverse=True) - log_fail
            log_w = jax.nn.log_sigmoid(z) + after + logsurv[..., None]
            w = jnp.where(valid, jnp.exp(log_w), 0.0)
            acc = acc + jnp.einsum('bhts,bhsd->bhtd', w, vb)
            logsurv = logsurv + jnp.sum(log_fail, axis=3)
            return acc, logsurv

        init = (jnp.zeros((B, H, SB_BLOCK, d), jnp.float32),
                jnp.zeros((B, H, SB_BLOCK), jnp.float32))
        acc, _ = lax.fori_loop(0, i + 1, body, init)
        return acc

    out = lax.map(one_query_block, (jnp.arange(nb, dtype=jnp.int32), q_blocks))
    return out.transpose(1, 2, 0, 3, 4).reshape(B, H, S, d).astype(v.dtype)


def gla_chunked(q, k, v, log_a):
    B, H, S, dk = q.shape
    dv = v.shape[-1]
    C = GLA_CHUNK
    nc = S // C

    def to_chunks(t):
        return t.astype(jnp.float32).reshape(B, H, nc, C, t.shape[-1]).transpose(2, 0, 1, 3, 4)

    qc, kc, vc, gc = to_chunks(q), to_chunks(k), to_chunks(v), to_chunks(log_a)
    causal = (jnp.arange(C)[None, :] <= jnp.arange(C)[:, None])[None, None, :, :, None]

    def step(state, inp):
        qb, kb, vb, gb = inp
        b = jnp.cumsum(gb, axis=2)
        diff = b[:, :, :, None, :] - b[:, :, None, :, :]
        decay = jnp.where(causal, jnp.exp(jnp.where(causal, diff, 0.0)), 0.0)
        scores = jnp.einsum('bhtd,bhsd,bhtsd->bhts', qb, kb, decay)
        o_intra = jnp.einsum('bhts,bhsv->bhtv', scores, vb)
        o_inter = jnp.einsum('bhtd,bhdv->bhtv', qb * jnp.exp(b), state)
        b_last = b[:, :, -1:, :]
        state = (jnp.exp(b_last[:, :, 0, :])[..., None] * state
                 + jnp.einsum('bhsd,bhsv->bhdv', kb * jnp.exp(b_last - b), vb))
        return state, o_intra + o_inter

    state0 = jnp.zeros((B, H, dk, dv), jnp.float32)
    _, out = lax.scan(step, state0, (qc, kc, vc, gc))
    return out.transpose(1, 2, 0, 3, 4).reshape(B, H, S, dv).astype(v.dtype)


def hybrid_layer(x, pre_norm_w, w_in, w_alpha2, b_alpha2, sb_norm_w, gla_norm_w, w_out, post_norm_w):
    h = rmsnorm(x, pre_norm_w)
    proj = jnp.einsum('bsd,de->bse', h, w_in)
    offsets = np.cumsum(np.array(IN_SIZES))[:-1].tolist()
    (sb_q, sb_k, sb_v, sb_gate,
     gla_q, gla_k, gla_v, gla_gate, gla_lr) = jnp.split(proj, offsets, axis=-1)

    sb_o = stick_breaking_attention(split_heads(sb_q, SB_HEADS),
                                    split_heads(sb_k, SB_HEADS),
                                    split_heads(sb_v, SB_HEADS))
    sb_y = head_rmsnorm(merge_heads(sb_o), sb_norm_w, SB_HEADS) * jax.nn.silu(sb_gate)

    gate_logits = jnp.einsum('bsr,rk->bsk', gla_lr, w_alpha2) + b_alpha2
    log_a = jax.nn.log_sigmoid(gate_logits.astype(jnp.float32)) / GATE_TEMP
    gla_o = gla_chunked(split_heads(gla_q * (GLA_KEY_DIM ** -0.5), GLA_HEADS),
                        split_heads(gla_k, GLA_HEADS),
                        split_heads(gla_v, GLA_HEADS),
                        split_heads(log_a, GLA_HEADS))
    gla_y = head_rmsnorm(merge_heads(gla_o), gla_norm_w, GLA_HEADS) * jax.nn.silu(gla_gate)

    y = jnp.einsum('bse,ed->bsd', jnp.concatenate([sb_y, gla_y], axis=-1), w_out)
    return x + rmsnorm(y, post_norm_w)


def setup_inputs(seed: int = 0) -> dict:
    key = jax.random.key(seed)
    ks = jax.random.split(key, 10)
    f32 = jnp.float32
    x = jax.random.normal(ks[0], (BATCH, SEQ, D_MODEL), f32)
    pre_norm_w = 1.0 + 0.02 * jax.random.normal(ks[1], (DEPTH, D_MODEL), f32)
    w_in = jax.random.normal(ks[2], (DEPTH, D_MODEL, IN_PROJ_WIDTH), f32) * D_MODEL ** -0.5
    w_alpha2 = jax.random.normal(ks[3], (DEPTH, GATE_RANK, GLA_KEY_WIDTH), f32) * GATE_RANK ** -0.5
    b_alpha2 = 0.1 * jax.random.normal(ks[4], (DEPTH, GLA_KEY_WIDTH), f32)
    sb_norm_w = 1.0 + 0.02 * jax.random.normal(ks[5], (DEPTH, SB_WIDTH), f32)
    gla_norm_w = 1.0 + 0.02 * jax.random.normal(ks[6], (DEPTH, GLA_WIDTH), f32)
    w_out = jax.random.normal(ks[7], (DEPTH, D_INNER, D_MODEL), f32) * D_INNER ** -0.5
    post_norm_w = 1.0 + 0.02 * jax.random.normal(ks[8], (DEPTH, D_MODEL), f32)
    return {"x": x, "pre_norm_w": pre_norm_w, "w_in": w_in, "w_alpha2": w_alpha2,
            "b_alpha2": b_alpha2, "sb_norm_w": sb_norm_w, "gla_norm_w": gla_norm_w,
            "w_out": w_out, "post_norm_w": post_norm_w}


def reference(x, pre_norm_w, w_in, w_alpha2, b_alpha2, sb_norm_w, gla_norm_w, w_out, post_norm_w):
    for layer in range(DEPTH):
        x = hybrid_layer(x, pre_norm_w[layer], w_in[layer], w_alpha2[layer], b_alpha2[layer],
                         sb_norm_w[layer], gla_norm_w[layer], w_out[layer], post_norm_w[layer])
    return x
```

```python
import functools
import math

import jax
import jax.numpy as jnp
from jax import lax
from jax.experimental import pallas as pl
from jax.experimental.pallas import tpu as pltpu

F32 = jnp.float32
BF16 = jnp.bfloat16

D_MODEL = 1024
SB_HEADS = 8
SB_HEAD_DIM = 128
SB_WIDTH = SB_HEADS * SB_HEAD_DIM
GLA_HEADS = 4
GLA_KEY_DIM = 128
GLA_VAL_DIM = 256
GLA_KEY_WIDTH = GLA_HEADS * GLA_KEY_DIM
GLA_WIDTH = GLA_HEADS * GLA_VAL_DIM
GATE_RANK = 16
GATE_TEMP = 16.0
NORM_EPS = 1e-6
LANES = 128

OFF_SB_Q = 0
OFF_SB_K = OFF_SB_Q + SB_WIDTH
OFF_SB_V = OFF_SB_K + SB_WIDTH
OFF_SB_G = OFF_SB_V + SB_WIDTH
OFF_GLA_Q = OFF_SB_G + SB_WIDTH
OFF_GLA_K = OFF_GLA_Q + GLA_KEY_WIDTH
OFF_GLA_V = OFF_GLA_K + GLA_KEY_WIDTH
OFF_GLA_G = OFF_GLA_V + GLA_WIDTH
OFF_GLA_LR = OFF_GLA_G + GLA_WIDTH
IN_PROJ_WIDTH = OFF_GLA_LR + GATE_RANK
SLAB_WIDTH = OFF_GLA_LR + LANES

LOG2E = math.log2(math.e)
LN2 = math.log(2.0)

INPROJ_TM = 512
INPROJ_TN = 512
SB_TILE = 256
GLA_CHUNK = 128
OUTPROJ_TM = 512
VMEM_LIMIT = 56 * 1024 * 1024

_NT = (((1,), (1,)), ((), ()))
_TN = (((0,), (0,)), ((), ()))


def _inproj_kernel(x_ref, nw_ref, w_ref, out_ref):
    x = x_ref[...]
    ms = jnp.mean(x * x, axis=-1, keepdims=True)
    h = (x * lax.rsqrt(ms + NORM_EPS) * nw_ref[...]).astype(BF16)
    for c0 in range(0, SLAB_WIDTH, INPROJ_TN):
        c1 = min(c0 + INPROJ_TN, SLAB_WIDTH)
        r = jnp.dot(h, w_ref[:, c0:c1], preferred_element_type=F32)
        if OFF_SB_Q <= c0 < OFF_SB_K:
            r = r * (LOG2E * SB_HEAD_DIM ** -0.5)
        elif OFF_GLA_Q <= c0 < OFF_GLA_K:
            r = r * (GLA_KEY_DIM ** -0.5)
        out_ref[:, c0:c1] = r.astype(BF16)


def _inproj(x2, pre_w, w_slab):
    m = x2.shape[0]
    return pl.pallas_call(
        _inproj_kernel,
        grid=(m // INPROJ_TM,),
        in_specs=[
            pl.BlockSpec((INPROJ_TM, D_MODEL), lambda i: (i, 0)),
            pl.BlockSpec((1, D_MODEL), lambda i: (0, 0)),
            pl.BlockSpec((D_MODEL, SLAB_WIDTH), lambda i: (0, 0),
                         pipeline_mode=pl.Buffered(1)),
        ],
        out_specs=pl.BlockSpec((INPROJ_TM, SLAB_WIDTH), lambda i: (i, 0)),
        out_shape=jax.ShapeDtypeStruct((m, SLAB_WIDTH), BF16),
        compiler_params=pltpu.CompilerParams(
            dimension_semantics=("arbitrary",), vmem_limit_bytes=VMEM_LIMIT),
        name="inproj",
    )(x2, pre_w, w_slab)


def _silu(g):
    return g / (1.0 + jnp.exp(-g))


def _sb_kernel(q_ref, k_ref, v_ref, g_ref, nw_ref, tri_ref, o_ref):
    t = SB_TILE
    i = pl.program_id(2)
    q = q_ref[0]
    tri = tri_ref[...]

    def tile(j, ls, acc, diag):
        start = pl.multiple_of(j * t, t)
        kj = k_ref[0, pl.ds(start, t), :]
        vj = v_ref[0, pl.ds(start, t), :]
        z = lax.dot_general(q, kj, _NT, preferred_element_type=F32)
        sp = jnp.maximum(z, 0.0) + jnp.log2(1.0 + jnp.exp2(-jnp.abs(z)))
        if diag:
            row = lax.broadcasted_iota(jnp.int32, (t, t), 0)
            col = lax.broadcasted_iota(jnp.int32, (t, t), 1)
            valid = col < row
            sp = jnp.where(valid, sp, 0.0)
        c = jnp.dot(sp.astype(BF16), tri, preferred_element_type=F32)
        p = jnp.exp2(z + c)
        if diag:
            p = jnp.where(valid, p, 0.0)
        pv = jnp.dot(p.astype(BF16), vj, preferred_element_type=F32)
        acc = acc + jnp.exp2(ls) * pv
        ls = ls + c[:, 0:1]
        return ls, acc

    ls0 = jnp.zeros((t, 1), F32)
    acc0 = jnp.zeros((t, SB_HEAD_DIM), F32)
    ls, acc = tile(i, ls0, acc0, True)

    def body(n, carry):
        return tile(i - n, carry[0], carry[1], False)

    ls, acc = lax.fori_loop(1, i + 1, body, (ls, acc))

    ms = jnp.mean(acc * acc, axis=-1, keepdims=True)
    y = acc * lax.rsqrt(ms + NORM_EPS) * nw_ref[0]
    o_ref[0] = (y * _silu(g_ref[0].astype(F32))).astype(BF16)


def _sb_attention(slab3, sb_norm_w3, tri):
    b, s, _ = slab3.shape
    t = SB_TILE
    qb, kb, vb, gb = (OFF_SB_Q // LANES, OFF_SB_K // LANES, OFF_SB_V // LANES, OFF_SB_G // LANES)
    return pl.pallas_call(
        _sb_kernel,
        grid=(b, SB_HEADS, s // t),
        in_specs=[
            pl.BlockSpec((1, t, SB_HEAD_DIM), lambda bi, h, i: (bi, i, qb + h)),
            pl.BlockSpec((1, s, SB_HEAD_DIM), lambda bi, h, i: (bi, 0, kb + h)),
            pl.BlockSpec((1, s, SB_HEAD_DIM), lambda bi, h, i: (bi, 0, vb + h)),
            pl.BlockSpec((1, t, SB_HEAD_DIM), lambda bi, h, i: (bi, i, gb + h)),
            pl.BlockSpec((1, 1, SB_HEAD_DIM), lambda bi, h, i: (h, 0, 0)),
            pl.BlockSpec((t, t), lambda bi, h, i: (0, 0)),
        ],
        out_specs=pl.BlockSpec((1, t, SB_HEAD_DIM), lambda bi, h, i: (bi, i, h)),
        out_shape=jax.ShapeDtypeStruct((b, s, SB_WIDTH), BF16),
        compiler_params=pltpu.CompilerParams(
            dimension_semantics=("arbitrary", "arbitrary", "arbitrary"),
            vmem_limit_bytes=VMEM_LIMIT),
        name="sb_attention",
    )(slab3, slab3, slab3, slab3, sb_norm_w3, tri)


def _gla_kernel(q_ref, k_ref, v_ref, g_ref, lr_ref, w2_ref, b2_ref, nw_ref, low_ref,
                o_ref, state_ref):
    c = GLA_CHUNK

    @pl.when(pl.program_id(1) == 0)
    def _():
        state_ref[...] = jnp.zeros_like(state_ref)

    lr = lr_ref[0]
    low = low_ref[...]
    row = lax.broadcasted_iota(jnp.int32, (c, c), 0)
    col = lax.broadcasted_iota(jnp.int32, (c, c), 1)
    causal = col <= row
    for h in range(GLA_HEADS):
        ks = slice(h * GLA_KEY_DIM, (h + 1) * GLA_KEY_DIM)
        vs = slice(h * GLA_VAL_DIM, (h + 1) * GLA_VAL_DIM)
        logits = jnp.dot(lr, w2_ref[:, ks], preferred_element_type=F32) + b2_ref[:, ks]
        g = -(jnp.maximum(-logits, 0.0) + jnp.log(1.0 + jnp.exp(-jnp.abs(logits)))) * (1.0 / GATE_TEMP)
        g_hi = g.astype(BF16)
        g_lo = (g - g_hi.astype(F32)).astype(BF16)
        bcum = (jnp.dot(low, g_hi, preferred_element_type=F32)
                + jnp.dot(low, g_lo, preferred_element_type=F32))
        b_last = bcum[c - 1:c, :]
        qh = q_ref[0, :, ks].astype(F32)
        kh = k_ref[0, :, ks].astype(F32)
        vh = v_ref[0, :, vs]
        q_t = (qh * jnp.exp(bcum)).astype(BF16)
        k_t = (kh * jnp.exp(-bcum)).astype(BF16)
        k_h = (kh * jnp.exp(b_last - bcum)).astype(BF16)
        sc = lax.dot_general(q_t, k_t, _NT, preferred_element_type=F32)
        sc = jnp.where(causal, sc, 0.0).astype(BF16)
        st = state_ref[h]
        o = (jnp.dot(sc, vh, preferred_element_type=F32)
             + lax.dot_general(q_t, st.astype(BF16), _NT, preferred_element_type=F32))
        state_ref[h] = st * jnp.exp(b_last) + lax.dot_general(
            vh, k_h, _TN, preferred_element_type=F32)
        ms = jnp.mean(o * o, axis=-1, keepdims=True)
        y = o * lax.rsqrt(ms + NORM_EPS) * nw_ref[:, vs]
        o_ref[0, :, vs] = (y * _silu(g_ref[0, :, vs].astype(F32))).astype(BF16)


def _gla(slab3, w2_pad, b2, gla_norm_w, low):
    b, s, _ = slab3.shape
    c = GLA_CHUNK
    qb, kb = OFF_GLA_Q // GLA_KEY_WIDTH, OFF_GLA_K // GLA_KEY_WIDTH
    vb, gb = OFF_GLA_V // GLA_WIDTH, OFF_GLA_G // GLA_WIDTH
    lb = OFF_GLA_LR // LANES
    return pl.pallas_call(
        _gla_kernel,
        grid=(b, s // c),
        in_specs=[
            pl.BlockSpec((1, c, GLA_KEY_WIDTH), lambda bi, i: (bi, i, qb)),
            pl.BlockSpec((1, c, GLA_KEY_WIDTH), lambda bi, i: (bi, i, kb)),
            pl.BlockSpec((1, c, GLA_WIDTH), lambda bi, i: (bi, i, vb)),
            pl.BlockSpec((1, c, GLA_WIDTH), lambda bi, i: (bi, i, gb)),
            pl.BlockSpec((1, c, LANES), lambda bi, i: (bi, i, lb)),
            pl.BlockSpec((LANES, GLA_KEY_WIDTH), lambda bi, i: (0, 0)),
            pl.BlockSpec((1, GLA_KEY_WIDTH), lambda bi, i: (0, 0)),
            pl.BlockSpec((1, GLA_WIDTH), lambda bi, i: (0, 0)),
            pl.BlockSpec((c, c), lambda bi, i: (0, 0)),
        ],
        out_specs=pl.BlockSpec((1, c, GLA_WIDTH), lambda bi, i: (bi, i, 0)),
        out_shape=jax.ShapeDtypeStruct((b, s, GLA_WIDTH), BF16),
        scratch_shapes=[pltpu.VMEM((GLA_HEADS, GLA_VAL_DIM, GLA_KEY_DIM), F32)],
        compiler_params=pltpu.CompilerParams(
            dimension_semantics=("arbitrary", "arbitrary"),
            vmem_limit_bytes=VMEM_LIMIT),
        name="gla",
    )(slab3, slab3, slab3, slab3, slab3, w2_pad, b2, gla_norm_w, low)


def _outproj_kernel(sb_ref, gla_ref, w_ref, x_ref, nw_ref, o_ref):
    y = (jnp.dot(sb_ref[...], w_ref[0:SB_WIDTH, :], preferred_element_type=F32)
         + jnp.dot(gla_ref[...], w_ref[SB_WIDTH:, :], preferred_element_type=F32))
    ms = jnp.mean(y * y, axis=-1, keepdims=True)
    o_ref[...] = x_ref[...] + y * lax.rsqrt(ms + NORM_EPS) * nw_ref[...]


def _outproj(sb_y, gla_y, w_out, x2, post_w):
    m = x2.shape[0]
    tm = OUTPROJ_TM
    return pl.pallas_call(
        _outproj_kernel,
        grid=(m // tm,),
        in_specs=[
            pl.BlockSpec((tm, SB_WIDTH), lambda i: (i, 0)),
            pl.BlockSpec((tm, GLA_WIDTH), lambda i: (i, 0)),
            pl.BlockSpec((SB_WIDTH + GLA_WIDTH, D_MODEL), lambda i: (0, 0)),
            pl.BlockSpec((tm, D_MODEL), lambda i: (i, 0)),
            pl.BlockSpec((1, D_MODEL), lambda i: (0, 0)),
        ],
        out_specs=pl.BlockSpec((tm, D_MODEL), lambda i: (i, 0)),
        out_shape=jax.ShapeDtypeStruct((m, D_MODEL), F32),
        compiler_params=pltpu.CompilerParams(
            dimension_semantics=("arbitrary",), vmem_limit_bytes=VMEM_LIMIT),
        name="outproj",
    )(sb_y, gla_y, w_out, x2, post_w)


def _layer(x, pre_norm_w, w_in, w_alpha2, b_alpha2, sb_norm_w, gla_norm_w, w_out, post_norm_w):
    b, s, d = x.shape
    m = b * s
    x2 = x.reshape(m, d)
    w_slab = jnp.pad(w_in, ((0, 0), (0, SLAB_WIDTH - IN_PROJ_WIDTH))).astype(BF16)
    slab = _inproj(x2, pre_norm_w.reshape(1, d), w_slab)
    slab3 = slab.reshape(b, s, SLAB_WIDTH)

    t = SB_TILE
    tri = -(jnp.arange(t)[:, None] >= jnp.arange(t)[None, :]).astype(BF16)
    sb_y = _sb_attention(slab3, sb_norm_w.reshape(SB_HEADS, 1, SB_HEAD_DIM), tri)

    c = GLA_CHUNK
    low = (jnp.arange(c)[:, None] >= jnp.arange(c)[None, :]).astype(BF16)
    w2_pad = jnp.pad(w_alpha2, ((0, LANES - GATE_RANK), (0, 0))).astype(BF16)
    gla_y = _gla(slab3, w2_pad, b_alpha2.reshape(1, GLA_KEY_WIDTH),
                 gla_norm_w.reshape(1, GLA_WIDTH), low)

    out = _outproj(sb_y.reshape(m, SB_WIDTH), gla_y.reshape(m, GLA_WIDTH),
                   w_out.astype(BF16), x2, post_norm_w.reshape(1, d))
    return out.reshape(b, s, d)


def kernel(x, pre_norm_w, w_in, w_alpha2, b_alpha2, sb_norm_w, gla_norm_w, w_out, post_norm_w):
    for layer in range(pre_norm_w.shape[0]):
        x = _layer(x, pre_norm_w[layer], w_in[layer], w_alpha2[layer], b_alpha2[layer],
                   sb_norm_w[layer], gla_norm_w[layer], w_out[layer], post_norm_w[layer])
    return x
```

```python
import math

import jax
import jax.numpy as jnp
from jax import lax
from jax.experimental import pallas as pl
from jax.experimental.pallas import tpu as pltpu

F32 = jnp.float32
BF16 = jnp.bfloat16

D_MODEL = 1024
SB_HEADS = 8
SB_HEAD_DIM = 128
SB_WIDTH = SB_HEADS * SB_HEAD_DIM
GLA_HEADS = 4
GLA_KEY_DIM = 128
GLA_VAL_DIM = 256
GLA_KEY_WIDTH = GLA_HEADS * GLA_KEY_DIM
GLA_WIDTH = GLA_HEADS * GLA_VAL_DIM
GATE_RANK = 16
GATE_TEMP = 16.0
NORM_EPS = 1e-6
LANES = 128

OFF_SB_Q = 0
OFF_SB_K = OFF_SB_Q + SB_WIDTH
OFF_SB_V = OFF_SB_K + SB_WIDTH
OFF_SB_G = OFF_SB_V + SB_WIDTH
OFF_GLA_Q = OFF_SB_G + SB_WIDTH
OFF_GLA_K = OFF_GLA_Q + GLA_KEY_WIDTH
OFF_GLA_V = OFF_GLA_K + GLA_KEY_WIDTH
OFF_GLA_G = OFF_GLA_V + GLA_WIDTH
OFF_GLA_LR = OFF_GLA_G + GLA_WIDTH
IN_PROJ_WIDTH = OFF_GLA_LR + GATE_RANK
SLAB_WIDTH = OFF_GLA_LR + LANES

LOG2E = math.log2(math.e)

INPROJ_TM = 512
INPROJ_TN = 512
SB_TILE = 256
SB_HPG = 4
SB_UNDERFLOW_LOG2 = -150.0
SB_MASKED_LOGIT = -1e30
SB_SOFTPLUS_CLAMP = 64.0
GLA_CHUNK = 128
OUTPROJ_TM = 512
VMEM_LIMIT = 56 * 1024 * 1024

_NT = (((1,), (1,)), ((), ()))
_TN = (((0,), (0,)), ((), ()))


def _inproj_kernel(x_ref, nw_ref, w_ref, out_ref):
    x = x_ref[...]
    ms = jnp.mean(x * x, axis=-1, keepdims=True)
    h = (x * lax.rsqrt(ms + NORM_EPS) * nw_ref[...]).astype(BF16)
    for c0 in range(0, SLAB_WIDTH, INPROJ_TN):
        c1 = min(c0 + INPROJ_TN, SLAB_WIDTH)
        r = jnp.dot(h, w_ref[:, c0:c1], preferred_element_type=F32)
        if OFF_SB_Q <= c0 < OFF_SB_K:
            r = r * (LOG2E * SB_HEAD_DIM ** -0.5)
        elif OFF_GLA_Q <= c0 < OFF_GLA_K:
            r = r * (GLA_KEY_DIM ** -0.5)
        out_ref[:, c0:c1] = r.astype(BF16)


def _inproj(x2, pre_w, w_slab):
    m = x2.shape[0]
    return pl.pallas_call(
        _inproj_kernel,
        grid=(m // INPROJ_TM,),
        in_specs=[
            pl.BlockSpec((INPROJ_TM, D_MODEL), lambda i: (i, 0)),
            pl.BlockSpec((1, D_MODEL), lambda i: (0, 0)),
            pl.BlockSpec((D_MODEL, SLAB_WIDTH), lambda i: (0, 0),
                         pipeline_mode=pl.Buffered(1)),
        ],
        out_specs=pl.BlockSpec((INPROJ_TM, SLAB_WIDTH), lambda i: (i, 0)),
        out_shape=jax.ShapeDtypeStruct((m, SLAB_WIDTH), BF16),
        compiler_params=pltpu.CompilerParams(
            dimension_semantics=("arbitrary",), vmem_limit_bytes=VMEM_LIMIT),
        name="inproj",
    )(x2, pre_w, w_slab)


def _silu(g):
    return g / (1.0 + jnp.exp(-g))


def _sb_tile(q, k, v, tri, bias):
    z = lax.dot_general(q, k, _NT, preferred_element_type=F32)
    if bias is not None:
        z = z + bias
    sp = jnp.maximum(z, jnp.log2(1.0 + jnp.exp2(jnp.minimum(z, SB_SOFTPLUS_CLAMP))))
    c = jnp.dot(sp.astype(BF16), tri, preferred_element_type=F32)
    p = jnp.exp2(z + c)
    pv = jnp.dot(p.astype(BF16), v, preferred_element_type=F32)
    return c[:, 0:1], pv


def _sb_kernel(q_ref, kd_ref, vd_ref, kp_ref, vp_ref, g_ref, nw_ref, tri_ref, bias_ref,
               slab_hbm, o_ref, acc_ref, ls_ref, kbuf, vbuf, sem):
    t = SB_TILE
    bi = pl.program_id(0)
    hg = pl.program_id(1)
    i = pl.program_id(2)
    has_prev = i > 0
    tri = tri_ref[...]

    for h in range(SB_HPG):
        cs = slice(h * SB_HEAD_DIM, (h + 1) * SB_HEAD_DIM)
        q = q_ref[0, :, cs]
        ls_d, pv_d = _sb_tile(q, kd_ref[0, :, cs], vd_ref[0, :, cs], tri, bias_ref[...])
        ls_p, pv_p = _sb_tile(q, kp_ref[0, :, cs], vp_ref[0, :, cs], tri, None)
        acc_ref[h] = pv_d + jnp.where(has_prev, jnp.exp2(ls_d), 0.0) * pv_p
        ls_ref[h] = ls_d + jnp.where(has_prev, ls_p, 0.0)

    more = jnp.logical_and(i >= 2, jnp.max(ls_ref[...]) > SB_UNDERFLOW_LOG2)

    @pl.when(more)
    def _():
        for h in range(SB_HPG):
            cs = slice(h * SB_HEAD_DIM, (h + 1) * SB_HEAD_DIM)
            head = hg * SB_HPG + h

            def tile_copies(j, head=head):
                rows = pl.ds(pl.multiple_of(j * t, t), t)
                kcol = pl.ds(pl.multiple_of(OFF_SB_K + head * SB_HEAD_DIM, SB_HEAD_DIM), SB_HEAD_DIM)
                vcol = pl.ds(pl.multiple_of(OFF_SB_V + head * SB_HEAD_DIM, SB_HEAD_DIM), SB_HEAD_DIM)
                return (pltpu.make_async_copy(slab_hbm.at[bi, rows, kcol], kbuf, sem.at[0]),
                        pltpu.make_async_copy(slab_hbm.at[bi, rows, vcol], vbuf, sem.at[1]))

            def cond(carry):
                j, ls, _ = carry
                return jnp.logical_and(j >= 0, jnp.max(ls) > SB_UNDERFLOW_LOG2)

            def body(carry, cs=cs, tile_copies=tile_copies):
                j, ls, acc = carry
                kc, vc = tile_copies(j)
                kc.start()
                vc.start()
                kc.wait()
                vc.wait()
                ls_j, pv = _sb_tile(q_ref[0, :, cs], kbuf[...], vbuf[...], tri, None)
                return j - 1, ls + ls_j, acc + jnp.exp2(ls) * pv

            _, _, acc = lax.while_loop(cond, body, (i - 2, ls_ref[h], acc_ref[h]))
            acc_ref[h] = acc

    for h in range(SB_HPG):
        cs = slice(h * SB_HEAD_DIM, (h + 1) * SB_HEAD_DIM)
        acc = acc_ref[h]
        ms = jnp.mean(acc * acc, axis=-1, keepdims=True)
        y = acc * lax.rsqrt(ms + NORM_EPS) * nw_ref[:, cs]
        o_ref[0, :, cs] = (y * _silu(g_ref[0, :, cs].astype(F32))).astype(BF16)


def _sb_attention(slab3, sb_norm_w, tri, bias):
    b, s, _ = slab3.shape
    t = SB_TILE
    gw = SB_HPG * SB_HEAD_DIM
    qb, kb, vb, gb = OFF_SB_Q // gw, OFF_SB_K // gw, OFF_SB_V // gw, OFF_SB_G // gw

    def tile_spec(cb):
        return pl.BlockSpec((1, t, gw), lambda bi, hg, i: (bi, i, cb + hg))

    def prev_spec(cb):
        return pl.BlockSpec((1, t, gw), lambda bi, hg, i: (bi, jnp.maximum(i - 1, 0), cb + hg))

    return pl.pallas_call(
        _sb_kernel,
        grid=(b, SB_HEADS // SB_HPG, s // t),
        in_specs=[
            tile_spec(qb), tile_spec(kb), tile_spec(vb), prev_spec(kb), prev_spec(vb), tile_spec(gb),
            pl.BlockSpec((1, gw), lambda bi, hg, i: (0, hg)),
            pl.BlockSpec((t, t), lambda bi, hg, i: (0, 0)),
            pl.BlockSpec((t, t), lambda bi, hg, i: (0, 0)),
            pl.BlockSpec(memory_space=pl.ANY),
        ],
        out_specs=pl.BlockSpec((1, t, gw), lambda bi, hg, i: (bi, i, hg)),
        out_shape=jax.ShapeDtypeStruct((b, s, SB_WIDTH), BF16),
        scratch_shapes=[
            pltpu.VMEM((SB_HPG, t, SB_HEAD_DIM), F32),
            pltpu.VMEM((SB_HPG, t, 1), F32),
            pltpu.VMEM((t, SB_HEAD_DIM), BF16),
            pltpu.VMEM((t, SB_HEAD_DIM), BF16),
            pltpu.SemaphoreType.DMA((2,)),
        ],
        compiler_params=pltpu.CompilerParams(
            dimension_semantics=("arbitrary", "arbitrary", "arbitrary"),
            vmem_limit_bytes=VMEM_LIMIT),
        name="sb_attention",
    )(slab3, slab3, slab3, slab3, slab3, slab3, sb_norm_w, tri, bias, slab3)


def _gla_kernel(q_ref, k_ref, v_ref, g_ref, lr_ref, w2_ref, b2_ref, nw_ref, low_ref,
                o_ref, state_ref):
    c = GLA_CHUNK

    @pl.when(pl.program_id(1) == 0)
    def _():
        state_ref[...] = jnp.zeros_like(state_ref)

    lr = lr_ref[0]
    low = low_ref[...]
    row = lax.broadcasted_iota(jnp.int32, (c, c), 0)
    col = lax.broadcasted_iota(jnp.int32, (c, c), 1)
    causal = col <= row
    for h in range(GLA_HEADS):
        ks = slice(h * GLA_KEY_DIM, (h + 1) * GLA_KEY_DIM)
        vs = slice(h * GLA_VAL_DIM, (h + 1) * GLA_VAL_DIM)
        logits = jnp.dot(lr, w2_ref[:, ks], preferred_element_type=F32) + b2_ref[:, ks]
        g = -(jnp.maximum(-logits, 0.0) + jnp.log(1.0 + jnp.exp(-jnp.abs(logits)))) * (1.0 / GATE_TEMP)
        g_hi = g.astype(BF16)
        g_lo = (g - g_hi.astype(F32)).astype(BF16)
        bcum = (jnp.dot(low, g_hi, preferred_element_type=F32)
                + jnp.dot(low, g_lo, preferred_element_type=F32))
        b_last = bcum[c - 1:c, :]
        qh = q_ref[0, :, ks].astype(F32)
        kh = k_ref[0, :, ks].astype(F32)
        vh = v_ref[0, :, vs]
        q_t = (qh * jnp.exp(bcum)).astype(BF16)
        k_t = (kh * jnp.exp(-bcum)).astype(BF16)
        k_h = (kh * jnp.exp(b_last - bcum)).astype(BF16)
        sc = lax.dot_general(q_t, k_t, _NT, preferred_element_type=F32)
        sc = jnp.where(causal, sc, 0.0).astype(BF16)
        st = state_ref[h]
        o = (jnp.dot(sc, vh, preferred_element_type=F32)
             + lax.dot_general(q_t, st.astype(BF16), _NT, preferred_element_type=F32))
        state_ref[h] = st * jnp.exp(b_last) + lax.dot_general(
            vh, k_h, _TN, preferred_element_type=F32)
        ms = jnp.mean(o * o, axis=-1, keepdims=True)
        y = o * lax.rsqrt(ms + NORM_EPS) * nw_ref[:, vs]
        o_ref[0, :, vs] = (y * _silu(g_ref[0, :, vs].astype(F32))).astype(BF16)


def _gla(slab3, w2_pad, b2, gla_norm_w, low):
    b, s, _ = slab3.shape
    c = GLA_CHUNK
    qb, kb = OFF_GLA_Q // GLA_KEY_WIDTH, OFF_GLA_K // GLA_KEY_WIDTH
    vb, gb = OFF_GLA_V // GLA_WIDTH, OFF_GLA_G // GLA_WIDTH
    lb = OFF_GLA_LR // LANES
    return pl.pallas_call(
        _gla_kernel,
        grid=(b, s // c),
        in_specs=[
            pl.BlockSpec((1, c, GLA_KEY_WIDTH), lambda bi, i: (bi, i, qb)),
            pl.BlockSpec((1, c, GLA_KEY_WIDTH), lambda bi, i: (bi, i, kb)),
            pl.BlockSpec((1, c, GLA_WIDTH), lambda bi, i: (bi, i, vb)),
            pl.BlockSpec((1, c, GLA_WIDTH), lambda bi, i: (bi, i, gb)),
            pl.BlockSpec((1, c, LANES), lambda bi, i: (bi, i, lb)),
            pl.BlockSpec((LANES, GLA_KEY_WIDTH), lambda bi, i: (0, 0)),
            pl.BlockSpec((1, GLA_KEY_WIDTH), lambda bi, i: (0, 0)),
            pl.BlockSpec((1, GLA_WIDTH), lambda bi, i: (0, 0)),
            pl.BlockSpec((c, c), lambda bi, i: (0, 0)),
        ],
        out_specs=pl.BlockSpec((1, c, GLA_WIDTH), lambda bi, i: (bi, i, 0)),
        out_shape=jax.ShapeDtypeStruct((b, s, GLA_WIDTH), BF16),
        scratch_shapes=[pltpu.VMEM((GLA_HEADS, GLA_VAL_DIM, GLA_KEY_DIM), F32)],
        compiler_params=pltpu.CompilerParams(
            dimension_semantics=("arbitrary", "arbitrary"),
            vmem_limit_bytes=VMEM_LIMIT),
        name="gla",
    )(slab3, slab3, slab3, slab3, slab3, w2_pad, b2, gla_norm_w, low)


def _outproj_kernel(sb_ref, gla_ref, w_ref, x_ref, nw_ref, o_ref):
    y = (jnp.dot(sb_ref[...], w_ref[0:SB_WIDTH, :], preferred_element_type=F32)
         + jnp.dot(gla_ref[...], w_ref[SB_WIDTH:, :], preferred_element_type=F32))
    ms = jnp.mean(y * y, axis=-1, keepdims=True)
    o_ref[...] = x_ref[...] + y * lax.rsqrt(ms + NORM_EPS) * nw_ref[...]


def _outproj(sb_y, gla_y, w_out, x2, post_w):
    m = x2.shape[0]
    tm = OUTPROJ_TM
    return pl.pallas_call(
        _outproj_kernel,
        grid=(m // tm,),
        in_specs=[
            pl.BlockSpec((tm, SB_WIDTH), lambda i: (i, 0)),
            pl.BlockSpec((tm, GLA_WIDTH), lambda i: (i, 0)),
            pl.BlockSpec((SB_WIDTH + GLA_WIDTH, D_MODEL), lambda i: (0, 0)),
            pl.BlockSpec((tm, D_MODEL), lambda i: (i, 0)),
            pl.BlockSpec((1, D_MODEL), lambda i: (0, 0)),
        ],
        out_specs=pl.BlockSpec((tm, D_MODEL), lambda i: (i, 0)),
        out_shape=jax.ShapeDtypeStruct((m, D_MODEL), F32),
        compiler_params=pltpu.CompilerParams(
            dimension_semantics=("arbitrary",), vmem_limit_bytes=VMEM_LIMIT),
        name="outproj",
    )(sb_y, gla_y, w_out, x2, post_w)


def _layer(x, pre_norm_w, w_in, w_alpha2, b_alpha2, sb_norm_w, gla_norm_w, w_out, post_norm_w):
    b, s, d = x.shape
    m = b * s
    x2 = x.reshape(m, d)
    w_slab = jnp.pad(w_in, ((0, 0), (0, SLAB_WIDTH - IN_PROJ_WIDTH))).astype(BF16)
    slab = _inproj(x2, pre_norm_w.reshape(1, d), w_slab)
    slab3 = slab.reshape(b, s, SLAB_WIDTH)

    t = SB_TILE
    pos = jnp.arange(t)
    tri = -(pos[:, None] >= pos[None, :]).astype(BF16)
    bias = jnp.where(pos[None, :] < pos[:, None], 0.0, SB_MASKED_LOGIT).astype(F32)
    sb_y = _sb_attention(slab3, sb_norm_w.reshape(1, SB_WIDTH), tri, bias)

    c = GLA_CHUNK
    low = (jnp.arange(c)[:, None] >= jnp.arange(c)[None, :]).astype(BF16)
    w2_pad = jnp.pad(w_alpha2, ((0, LANES - GATE_RANK), (0, 0))).astype(BF16)
    gla_y = _gla(slab3, w2_pad, b_alpha2.reshape(1, GLA_KEY_WIDTH),
                 gla_norm_w.reshape(1, GLA_WIDTH), low)

    out = _outproj(sb_y.reshape(m, SB_WIDTH), gla_y.reshape(m, GLA_WIDTH),
                   w_out.astype(BF16), x2, post_norm_w.reshape(1, d))
    return out.reshape(b, s, d)


def kernel(x, pre_norm_w, w_in, w_alpha2, b_alpha2, sb_norm_w, gla_norm_w, w_out, post_norm_w):
    for layer in range(pre_norm_w.shape[0]):
        x = _layer(x, pre_norm_w[layer], w_in[layer], w_alpha2[layer], b_alpha2[layer],
                   sb_norm_w[layer], gla_norm_w[layer], w_out[layer], post_norm_w[layer])
    return x
```

```python
import math

import jax
import jax.numpy as jnp
from jax import lax
from jax.experimental import pallas as pl
from jax.experimental.pallas import tpu as pltpu

F32 = jnp.float32
BF16 = jnp.bfloat16

D_MODEL = 1024
SB_HEADS = 8
SB_HEAD_DIM = 128
SB_WIDTH = SB_HEADS * SB_HEAD_DIM
GLA_HEADS = 4
GLA_KEY_DIM = 128
GLA_VAL_DIM = 256
GLA_KEY_WIDTH = GLA_HEADS * GLA_KEY_DIM
GLA_WIDTH = GLA_HEADS * GLA_VAL_DIM
GATE_RANK = 16
GATE_TEMP = 16.0
NORM_EPS = 1e-6
LANES = 128

OFF_SB_Q = 0
OFF_SB_K = OFF_SB_Q + SB_WIDTH
OFF_SB_V = OFF_SB_K + SB_WIDTH
OFF_SB_G = OFF_SB_V + SB_WIDTH
OFF_GLA_Q = OFF_SB_G + SB_WIDTH
OFF_GLA_K = OFF_GLA_Q + GLA_KEY_WIDTH
OFF_GLA_V = OFF_GLA_K + GLA_KEY_WIDTH
OFF_GLA_G = OFF_GLA_V + GLA_WIDTH
OFF_GLA_LR = OFF_GLA_G + GLA_WIDTH
IN_PROJ_WIDTH = OFF_GLA_LR + GATE_RANK
SLAB_WIDTH = OFF_GLA_LR + LANES

LOG2E = math.log2(math.e)

INPROJ_TM = 512
INPROJ_TN = 512
SB_TILE = 256
SB_HPG = 4
SB_UNDERFLOW_LOG2 = -150.0
SB_MASKED_LOGIT = -1e30
SB_SOFTPLUS_CLAMP = 64.0
GLA_CHUNK = 128
OUTPROJ_TM = 512
VMEM_LIMIT = 56 * 1024 * 1024

_NT = (((1,), (1,)), ((), ()))
_TN = (((0,), (0,)), ((), ()))


def _inproj_kernel(x_ref, nw_ref, w_ref, out_ref):
    x = x_ref[...]
    ms = jnp.mean(x * x, axis=-1, keepdims=True)
    h = (x * lax.rsqrt(ms + NORM_EPS) * nw_ref[...]).astype(BF16)
    for c0 in range(0, SLAB_WIDTH, INPROJ_TN):
        c1 = min(c0 + INPROJ_TN, SLAB_WIDTH)
        r = jnp.dot(h, w_ref[:, c0:c1], preferred_element_type=F32)
        if OFF_SB_Q <= c0 < OFF_SB_K:
            r = r * (LOG2E * SB_HEAD_DIM ** -0.5)
        elif OFF_GLA_Q <= c0 < OFF_GLA_K:
            r = r * (GLA_KEY_DIM ** -0.5)
        out_ref[:, c0:c1] = r.astype(BF16)


def _inproj(x2, pre_w, w_slab):
    m = x2.shape[0]
    return pl.pallas_call(
        _inproj_kernel,
        grid=(m // INPROJ_TM,),
        in_specs=[
            pl.BlockSpec((INPROJ_TM, D_MODEL), lambda i: (i, 0)),
            pl.BlockSpec((1, D_MODEL), lambda i: (0, 0)),
            pl.BlockSpec((D_MODEL, SLAB_WIDTH), lambda i: (0, 0),
                         pipeline_mode=pl.Buffered(1)),
        ],
        out_specs=pl.BlockSpec((INPROJ_TM, SLAB_WIDTH), lambda i: (i, 0)),
        out_shape=jax.ShapeDtypeStruct((m, SLAB_WIDTH), BF16),
        compiler_params=pltpu.CompilerParams(
            dimension_semantics=("arbitrary",), vmem_limit_bytes=VMEM_LIMIT),
        name="inproj",
    )(x2, pre_w, w_slab)


def _silu(g):
    return g / (1.0 + jnp.exp(-g))


def _sb_logits(q, k, bias):
    z = lax.dot_general(q, k, _NT, preferred_element_type=F32)
    if bias is not None:
        z = z + bias
    sp = jnp.maximum(z, jnp.log2(1.0 + jnp.exp2(jnp.minimum(z, SB_SOFTPLUS_CLAMP))))
    return z, sp.astype(BF16)


def _sb_weights(z, sp, tri):
    c = jnp.dot(sp, tri, preferred_element_type=F32)
    return c[:, 0:1], jnp.exp2(z + c).astype(BF16)


def _sb_tile(q, k, v, tri, bias):
    z, sp = _sb_logits(q, k, bias)
    ls, p = _sb_weights(z, sp, tri)
    return ls, jnp.dot(p, v, preferred_element_type=F32)


def _sb_kernel(q_ref, kd_ref, vd_ref, kp_ref, vp_ref, g_ref, nw_ref, tri_ref, bias_ref,
               slab_hbm, o_ref, acc_ref, ls_ref, kbuf, vbuf, sem):
    t = SB_TILE
    bi = pl.program_id(0)
    hg = pl.program_id(1)
    i = pl.program_id(2)
    has_prev = i > 0
    tri = tri_ref[...]

    def cols(h):
        return slice(h * SB_HEAD_DIM, (h + 1) * SB_HEAD_DIM)

    logits, weights = {}, {}
    for step in range(SB_HPG + 2):
        if step < SB_HPG:
            cs = cols(step)
            q = q_ref[0, :, cs]
            logits[step] = (_sb_logits(q, kd_ref[0, :, cs], bias_ref[...]),
                            _sb_logits(q, kp_ref[0, :, cs], None))
        if 0 <= step - 1 < SB_HPG:
            (z_d, sp_d), (z_p, sp_p) = logits.pop(step - 1)
            weights[step - 1] = (_sb_weights(z_d, sp_d, tri), _sb_weights(z_p, sp_p, tri))
        if 0 <= step - 2 < SB_HPG:
            h = step - 2
            cs = cols(h)
            (ls_d, p_d), (ls_p, p_p) = weights.pop(h)
            pv_d = jnp.dot(p_d, vd_ref[0, :, cs], preferred_element_type=F32)
            pv_p = jnp.dot(p_p, vp_ref[0, :, cs], preferred_element_type=F32)
            acc_ref[h] = pv_d + jnp.where(has_prev, jnp.exp2(ls_d), 0.0) * pv_p
            ls_ref[h] = ls_d + jnp.where(has_prev, ls_p, 0.0)

    more = jnp.logical_and(i >= 2, jnp.max(ls_ref[...]) > SB_UNDERFLOW_LOG2)

    @pl.when(more)
    def _():
        for h in range(SB_HPG):
            cs = slice(h * SB_HEAD_DIM, (h + 1) * SB_HEAD_DIM)
            head = hg * SB_HPG + h

            def tile_copies(j, head=head):
                rows = pl.ds(pl.multiple_of(j * t, t), t)
                kcol = pl.ds(pl.multiple_of(OFF_SB_K + head * SB_HEAD_DIM, SB_HEAD_DIM), SB_HEAD_DIM)
                vcol = pl.ds(pl.multiple_of(OFF_SB_V + head * SB_HEAD_DIM, SB_HEAD_DIM), SB_HEAD_DIM)
                return (pltpu.make_async_copy(slab_hbm.at[bi, rows, kcol], kbuf, sem.at[0]),
                        pltpu.make_async_copy(slab_hbm.at[bi, rows, vcol], vbuf, sem.at[1]))

            def cond(carry):
                j, ls, _ = carry
                return jnp.logical_and(j >= 0, jnp.max(ls) > SB_UNDERFLOW_LOG2)

            def body(carry, cs=cs, tile_copies=tile_copies):
                j, ls, acc = carry
                kc, vc = tile_copies(j)
                kc.start()
                vc.start()
                kc.wait()
                vc.wait()
                ls_j, pv = _sb_tile(q_ref[0, :, cs], kbuf[...], vbuf[...], tri, None)
                return j - 1, ls + ls_j, acc + jnp.exp2(ls) * pv

            _, _, acc = lax.while_loop(cond, body, (i - 2, ls_ref[h], acc_ref[h]))
            acc_ref[h] = acc

    for h in range(SB_HPG):
        cs = slice(h * SB_HEAD_DIM, (h + 1) * SB_HEAD_DIM)
        acc = acc_ref[h]
        ms = jnp.mean(acc * acc, axis=-1, keepdims=True)
        y = acc * lax.rsqrt(ms + NORM_EPS) * nw_ref[:, cs]
        o_ref[0, :, cs] = (y * _silu(g_ref[0, :, cs].astype(F32))).astype(BF16)


def _sb_attention(slab3, sb_norm_w, tri, bias):
    b, s, _ = slab3.shape
    t = SB_TILE
    gw = SB_HPG * SB_HEAD_DIM
    qb, kb, vb, gb = OFF_SB_Q // gw, OFF_SB_K // gw, OFF_SB_V // gw, OFF_SB_G // gw

    def tile_spec(cb):
        return pl.BlockSpec((1, t, gw), lambda bi, hg, i: (bi, i, cb + hg))

    def prev_spec(cb):
        return pl.BlockSpec((1, t, gw), lambda bi, hg, i: (bi, jnp.maximum(i - 1, 0), cb + hg))

    return pl.pallas_call(
        _sb_kernel,
        grid=(b, SB_HEADS // SB_HPG, s // t),
        in_specs=[
            tile_spec(qb), tile_spec(kb), tile_spec(vb), prev_spec(kb), prev_spec(vb), tile_spec(gb),
            pl.BlockSpec((1, gw), lambda bi, hg, i: (0, hg)),
            pl.BlockSpec((t, t), lambda bi, hg, i: (0, 0)),
            pl.BlockSpec((t, t), lambda bi, hg, i: (0, 0)),
            pl.BlockSpec(memory_space=pl.ANY),
        ],
        out_specs=pl.BlockSpec((1, t, gw), lambda bi, hg, i: (bi, i, hg)),
        out_shape=jax.ShapeDtypeStruct((b, s, SB_WIDTH), BF16),
        scratch_shapes=[
            pltpu.VMEM((SB_HPG, t, SB_HEAD_DIM), F32),
            pltpu.VMEM((SB_HPG, t, 1), F32),
            pltpu.VMEM((t, SB_HEAD_DIM), BF16),
            pltpu.VMEM((t, SB_HEAD_DIM), BF16),
            pltpu.SemaphoreType.DMA((2,)),
        ],
        compiler_params=pltpu.CompilerParams(
            dimension_semantics=("arbitrary", "arbitrary", "arbitrary"),
            vmem_limit_bytes=VMEM_LIMIT),
        name="sb_attention",
    )(slab3, slab3, slab3, slab3, slab3, slab3, sb_norm_w, tri, bias, slab3)


def _gla_kernel(q_ref, k_ref, v_ref, g_ref, lr_ref, w2_ref, b2_ref, nw_ref, low_ref,
                o_ref, state_ref):
    c = GLA_CHUNK

    @pl.when(pl.program_id(1) == 0)
    def _():
        state_ref[...] = jnp.zeros_like(state_ref)

    lr = lr_ref[0]
    low = low_ref[...]
    row = lax.broadcasted_iota(jnp.int32, (c, c), 0)
    col = lax.broadcasted_iota(jnp.int32, (c, c), 1)
    causal = col <= row
    for h in range(GLA_HEADS):
        ks = slice(h * GLA_KEY_DIM, (h + 1) * GLA_KEY_DIM)
        vs = slice(h * GLA_VAL_DIM, (h + 1) * GLA_VAL_DIM)
        logits = jnp.dot(lr, w2_ref[:, ks], preferred_element_type=F32) + b2_ref[:, ks]
        g = -(jnp.maximum(-logits, 0.0) + jnp.log(1.0 + jnp.exp(-jnp.abs(logits)))) * (1.0 / GATE_TEMP)
        g_hi = g.astype(BF16)
        g_lo = (g - g_hi.astype(F32)).astype(BF16)
        bcum = (jnp.dot(low, g_hi, preferred_element_type=F32)
                + jnp.dot(low, g_lo, preferred_element_type=F32))
        b_last = bcum[c - 1:c, :]
        qh = q_ref[0, :, ks].astype(F32)
        kh = k_ref[0, :, ks].astype(F32)
        vh = v_ref[0, :, vs]
        q_t = (qh * jnp.exp(bcum)).astype(BF16)
        k_t = (kh * jnp.exp(-bcum)).astype(BF16)
        k_h = (kh * jnp.exp(b_last - bcum)).astype(BF16)
        sc = lax.dot_general(q_t, k_t, _NT, preferred_element_type=F32)
        sc = jnp.where(causal, sc, 0.0).astype(BF16)
        st = state_ref[h]
        o = (jnp.dot(sc, vh, preferred_element_type=F32)
             + lax.dot_general(q_t, st.astype(BF16), _NT, preferred_element_type=F32))
        state_ref[h] = st * jnp.exp(b_last) + lax.dot_general(
            vh, k_h, _TN, preferred_element_type=F32)
        ms = jnp.mean(o * o, axis=-1, keepdims=True)
        y = o * lax.rsqrt(ms + NORM_EPS) * nw_ref[:, vs]
        o_ref[0, :, vs] = (y * _silu(g_ref[0, :, vs].astype(F32))).astype(BF16)


def _gla(slab3, w2_pad, b2, gla_norm_w, low):
    b, s, _ = slab3.shape
    c = GLA_CHUNK
    qb, kb = OFF_GLA_Q // GLA_KEY_WIDTH, OFF_GLA_K // GLA_KEY_WIDTH
    vb, gb = OFF_GLA_V // GLA_WIDTH, OFF_GLA_G // GLA_WIDTH
    lb = OFF_GLA_LR // LANES
    return pl.pallas_call(
        _gla_kernel,
        grid=(b, s // c),
        in_specs=[
            pl.BlockSpec((1, c, GLA_KEY_WIDTH), lambda bi, i: (bi, i, qb)),
            pl.BlockSpec((1, c, GLA_KEY_WIDTH), lambda bi, i: (bi, i, kb)),
            pl.BlockSpec((1, c, GLA_WIDTH), lambda bi, i: (bi, i, vb)),
            pl.BlockSpec((1, c, GLA_WIDTH), lambda bi, i: (bi, i, gb)),
            pl.BlockSpec((1, c, LANES), lambda bi, i: (bi, i, lb)),
            pl.BlockSpec((LANES, GLA_KEY_WIDTH), lambda bi, i: (0, 0)),
            pl.BlockSpec((1, GLA_KEY_WIDTH), lambda bi, i: (0, 0)),
            pl.BlockSpec((1, GLA_WIDTH), lambda bi, i: (0, 0)),
            pl.BlockSpec((c, c), lambda bi, i: (0, 0)),
        ],
        out_specs=pl.BlockSpec((1, c, GLA_WIDTH), lambda bi, i: (bi, i, 0)),
        out_shape=jax.ShapeDtypeStruct((b, s, GLA_WIDTH), BF16),
        scratch_shapes=[pltpu.VMEM((GLA_HEADS, GLA_VAL_DIM, GLA_KEY_DIM), F32)],
        compiler_params=pltpu.CompilerParams(
            dimension_semantics=("arbitrary", "arbitrary"),
            vmem_limit_bytes=VMEM_LIMIT),
        name="gla",
    )(slab3, slab3, slab3, slab3, slab3, w2_pad, b2, gla_norm_w, low)


def _outproj_kernel(sb_ref, gla_ref, w_ref, x_ref, nw_ref, o_ref):
    y = (jnp.dot(sb_ref[...], w_ref[0:SB_WIDTH, :], preferred_element_type=F32)
         + jnp.dot(gla_ref[...], w_ref[SB_WIDTH:, :], preferred_element_type=F32))
    ms = jnp.mean(y * y, axis=-1, keepdims=True)
    o_ref[...] = x_ref[...] + y * lax.rsqrt(ms + NORM_EPS) * nw_ref[...]


def _outproj(sb_y, gla_y, w_out, x2, post_w):
    m = x2.shape[0]
    tm = OUTPROJ_TM
    return pl.pallas_call(
        _outproj_kernel,
        grid=(m // tm,),
        in_specs=[
            pl.BlockSpec((tm, SB_WIDTH), lambda i: (i, 0)),
            pl.BlockSpec((tm, GLA_WIDTH), lambda i: (i, 0)),
            pl.BlockSpec((SB_WIDTH + GLA_WIDTH, D_MODEL), lambda i: (0, 0)),
            pl.BlockSpec((tm, D_MODEL), lambda i: (i, 0)),
            pl.BlockSpec((1, D_MODEL), lambda i: (0, 0)),
        ],
        out_specs=pl.BlockSpec((tm, D_MODEL), lambda i: (i, 0)),
        out_shape=jax.ShapeDtypeStruct((m, D_MODEL), F32),
        compiler_params=pltpu.CompilerParams(
            dimension_semantics=("arbitrary",), vmem_limit_bytes=VMEM_LIMIT),
        name="outproj",
    )(sb_y, gla_y, w_out, x2, post_w)


def _layer(x, pre_norm_w, w_in, w_alpha2, b_alpha2, sb_norm_w, gla_norm_w, w_out, post_norm_w):
    b, s, d = x.shape
    m = b * s
    x2 = x.reshape(m, d)
    w_slab = jnp.pad(w_in, ((0, 0), (0, SLAB_WIDTH - IN_PROJ_WIDTH))).astype(BF16)
    slab = _inproj(x2, pre_norm_w.reshape(1, d), w_slab)
    slab3 = slab.reshape(b, s, SLAB_WIDTH)

    t = SB_TILE
    pos = jnp.arange(t)
    tri = -(pos[:, None] >= pos[None, :]).astype(BF16)
    bias = jnp.where(pos[None, :] < pos[:, None], 0.0, SB_MASKED_LOGIT).astype(F32)
    sb_y = _sb_attention(slab3, sb_norm_w.reshape(1, SB_WIDTH), tri, bias)

    c = GLA_CHUNK
    low = (jnp.arange(c)[:, None] >= jnp.arange(c)[None, :]).astype(BF16)
    w2_pad = jnp.pad(w_alpha2, ((0, LANES - GATE_RANK), (0, 0))).astype(BF16)
    gla_y = _gla(slab3, w2_pad, b_alpha2.reshape(1, GLA_KEY_WIDTH),
                 gla_norm_w.reshape(1, GLA_WIDTH), low)

    out = _outproj(sb_y.reshape(m, SB_WIDTH), gla_y.reshape(m, GLA_WIDTH),
                   w_out.astype(BF16), x2, post_norm_w.reshape(1, d))
    return out.reshape(b, s, d)


def kernel(x, pre_norm_w, w_in, w_alpha2, b_alpha2, sb_norm_w, gla_norm_w, w_out, post_norm_w):
    for layer in range(pre_norm_w.shape[0]):
        x = _layer(x, pre_norm_w[layer], w_in[layer], w_alpha2[layer], b_alpha2[layer],
                   sb_norm_w[layer], gla_norm_w[layer], w_out[layer], post_norm_w[layer])
    return x
```

```python
import math

import jax
import jax.numpy as jnp
from jax import lax
from jax.experimental import pallas as pl
from jax.experimental.pallas import tpu as pltpu

F32 = jnp.float32
BF16 = jnp.bfloat16

D_MODEL = 1024
SB_HEADS = 8
SB_HEAD_DIM = 128
SB_WIDTH = SB_HEADS * SB_HEAD_DIM
GLA_HEADS = 4
GLA_KEY_DIM = 128
GLA_VAL_DIM = 256
GLA_KEY_WIDTH = GLA_HEADS * GLA_KEY_DIM
GLA_WIDTH = GLA_HEADS * GLA_VAL_DIM
GATE_RANK = 16
GATE_TEMP = 16.0
NORM_EPS = 1e-6
LANES = 128

OFF_SB_Q = 0
OFF_SB_K = OFF_SB_Q + SB_WIDTH
OFF_SB_V = OFF_SB_K + SB_WIDTH
OFF_SB_G = OFF_SB_V + SB_WIDTH
OFF_GLA_Q = OFF_SB_G + SB_WIDTH
OFF_GLA_K = OFF_GLA_Q + GLA_KEY_WIDTH
OFF_GLA_V = OFF_GLA_K + GLA_KEY_WIDTH
OFF_GLA_G = OFF_GLA_V + GLA_WIDTH
OFF_GLA_LR = OFF_GLA_G + GLA_WIDTH
IN_PROJ_WIDTH = OFF_GLA_LR + GATE_RANK
SLAB_WIDTH = OFF_GLA_LR + LANES

LOG2E = math.log2(math.e)

INPROJ_TM = 512
INPROJ_TN = 512
SB_TILE = 256
SB_HPG = 4
SB_UNDERFLOW_LOG2 = -150.0
SB_MASKED_LOGIT = -1e30
SB_SOFTPLUS_CLAMP = 64.0
GLA_CHUNK = 128
GLA_CPS = 2
OUTPROJ_TM = 512
VMEM_LIMIT = 56 * 1024 * 1024

_NT = (((1,), (1,)), ((), ()))
_TN = (((0,), (0,)), ((), ()))


def _inproj_kernel(x_ref, nw_ref, w_ref, out_ref):
    x = x_ref[...]
    ms = jnp.mean(x * x, axis=-1, keepdims=True)
    h = (x * lax.rsqrt(ms + NORM_EPS) * nw_ref[...]).astype(BF16)
    for c0 in range(0, SLAB_WIDTH, INPROJ_TN):
        c1 = min(c0 + INPROJ_TN, SLAB_WIDTH)
        r = jnp.dot(h, w_ref[:, c0:c1], preferred_element_type=F32)
        if OFF_SB_Q <= c0 < OFF_SB_K:
            r = r * (LOG2E * SB_HEAD_DIM ** -0.5)
        elif OFF_GLA_Q <= c0 < OFF_GLA_K:
            r = r * (GLA_KEY_DIM ** -0.5)
        elif OFF_SB_G <= c0 < OFF_GLA_Q or OFF_GLA_G <= c0 < OFF_GLA_LR:
            r = r * (1.0 / (1.0 + jnp.exp2(r * (-LOG2E))))
        out_ref[:, c0:c1] = r.astype(BF16)


def _inproj(x2, pre_w, w_slab):
    m = x2.shape[0]
    return pl.pallas_call(
        _inproj_kernel,
        grid=(m // INPROJ_TM,),
        in_specs=[
            pl.BlockSpec((INPROJ_TM, D_MODEL), lambda i: (i, 0)),
            pl.BlockSpec((1, D_MODEL), lambda i: (0, 0)),
            pl.BlockSpec((D_MODEL, SLAB_WIDTH), lambda i: (0, 0),
                         pipeline_mode=pl.Buffered(1)),
        ],
        out_specs=pl.BlockSpec((INPROJ_TM, SLAB_WIDTH), lambda i: (i, 0)),
        out_shape=jax.ShapeDtypeStruct((m, SLAB_WIDTH), BF16),
        compiler_params=pltpu.CompilerParams(
            dimension_semantics=("arbitrary",), vmem_limit_bytes=VMEM_LIMIT),
        name="inproj",
    )(x2, pre_w, w_slab)


def _sb_logits(q, k, bias):
    z = lax.dot_general(q, k, _NT, preferred_element_type=F32)
    if bias is not None:
        z = z + bias
    sp = jnp.maximum(z, jnp.log2(1.0 + jnp.exp2(jnp.minimum(z, SB_SOFTPLUS_CLAMP))))
    return z, sp.astype(BF16)


def _sb_weights(z, sp, tri):
    c = jnp.dot(sp, tri, preferred_element_type=F32)
    return c[:, 0:1], jnp.exp2(z + c).astype(BF16)


def _sb_tile(q, k, v, tri, bias):
    z, sp = _sb_logits(q, k, bias)
    ls, p = _sb_weights(z, sp, tri)
    return ls, jnp.dot(p, v, preferred_element_type=F32)


def _sb_kernel(q_ref, kd_ref, vd_ref, kp_ref, vp_ref, tri_ref, bias_ref,
               slab_hbm, o_ref, acc_ref, ls_ref, kbuf, vbuf, sem):
    t = SB_TILE
    bi = pl.program_id(0)
    hg = pl.program_id(1)
    i = pl.program_id(2)
    has_prev = i > 0
    tri = tri_ref[...]

    def cols(h):
        return slice(h * SB_HEAD_DIM, (h + 1) * SB_HEAD_DIM)

    logits, weights = {}, {}
    for step in range(SB_HPG + 2):
        if step < SB_HPG:
            cs = cols(step)
            q = q_ref[0, :, cs]
            logits[step] = (_sb_logits(q, kd_ref[0, :, cs], bias_ref[...]),
                            _sb_logits(q, kp_ref[0, :, cs], None))
        if 0 <= step - 1 < SB_HPG:
            (z_d, sp_d), (z_p, sp_p) = logits.pop(step - 1)
            weights[step - 1] = (_sb_weights(z_d, sp_d, tri), _sb_weights(z_p, sp_p, tri))
        if 0 <= step - 2 < SB_HPG:
            h = step - 2
            cs = cols(h)
            (ls_d, p_d), (ls_p, p_p) = weights.pop(h)
            pv_d = jnp.dot(p_d, vd_ref[0, :, cs], preferred_element_type=F32)
            pv_p = jnp.dot(p_p, vp_ref[0, :, cs], preferred_element_type=F32)
            acc_ref[h] = pv_d + jnp.where(has_prev, jnp.exp2(ls_d), 0.0) * pv_p
            ls_ref[h] = ls_d + jnp.where(has_prev, ls_p, 0.0)

    more = jnp.logical_and(i >= 2, jnp.max(ls_ref[...]) > SB_UNDERFLOW_LOG2)

    @pl.when(more)
    def _():
        for h in range(SB_HPG):
            cs = cols(h)
            head = hg * SB_HPG + h

            def tile_copies(j, head=head):
                rows = pl.ds(pl.multiple_of(j * t, t), t)
                kcol = pl.ds(pl.multiple_of(OFF_SB_K + head * SB_HEAD_DIM, SB_HEAD_DIM), SB_HEAD_DIM)
                vcol = pl.ds(pl.multiple_of(OFF_SB_V + head * SB_HEAD_DIM, SB_HEAD_DIM), SB_HEAD_DIM)
                return (pltpu.make_async_copy(slab_hbm.at[bi, rows, kcol], kbuf, sem.at[0]),
                        pltpu.make_async_copy(slab_hbm.at[bi, rows, vcol], vbuf, sem.at[1]))

            def cond(carry):
                j, ls, _ = carry
                return jnp.logical_and(j >= 0, jnp.max(ls) > SB_UNDERFLOW_LOG2)

            def body(carry, cs=cs, tile_copies=tile_copies):
                j, ls, acc = carry
                kc, vc = tile_copies(j)
                kc.start()
                vc.start()
                kc.wait()
                vc.wait()
                ls_j, pv = _sb_tile(q_ref[0, :, cs], kbuf[...], vbuf[...], tri, None)
                return j - 1, ls + ls_j, acc + jnp.exp2(ls) * pv

            _, _, acc = lax.while_loop(cond, body, (i - 2, ls_ref[h], acc_ref[h]))
            acc_ref[h] = acc

    for h in range(SB_HPG):
        o_ref[0, :, cols(h)] = acc_ref[h].astype(BF16)


def _sb_attention(slab3, tri, bias):
    b, s, _ = slab3.shape
    t = SB_TILE
    gw = SB_HPG * SB_HEAD_DIM
    qb, kb, vb = OFF_SB_Q // gw, OFF_SB_K // gw, OFF_SB_V // gw

    def tile_spec(cb):
        return pl.BlockSpec((1, t, gw), lambda bi, hg, i: (bi, i, cb + hg))

    def prev_spec(cb):
        return pl.BlockSpec((1, t, gw), lambda bi, hg, i: (bi, jnp.maximum(i - 1, 0), cb + hg))

    return pl.pallas_call(
        _sb_kernel,
        grid=(b, SB_HEADS // SB_HPG, s // t),
        in_specs=[
            tile_spec(qb), tile_spec(kb), tile_spec(vb), prev_spec(kb), prev_spec(vb),
            pl.BlockSpec((t, t), lambda bi, hg, i: (0, 0)),
            pl.BlockSpec((t, t), lambda bi, hg, i: (0, 0)),
            pl.BlockSpec(memory_space=pl.ANY),
        ],
        out_specs=pl.BlockSpec((1, t, gw), lambda bi, hg, i: (bi, i, hg)),
        out_shape=jax.ShapeDtypeStruct((b, s, SB_WIDTH), BF16),
        scratch_shapes=[
            pltpu.VMEM((SB_HPG, t, SB_HEAD_DIM), F32),
            pltpu.VMEM((SB_HPG, t, 1), F32),
            pltpu.VMEM((t, SB_HEAD_DIM), BF16),
            pltpu.VMEM((t, SB_HEAD_DIM), BF16),
            pltpu.SemaphoreType.DMA((2,)),
        ],
        compiler_params=pltpu.CompilerParams(
            dimension_semantics=("arbitrary", "arbitrary", "arbitrary"),
            vmem_limit_bytes=VMEM_LIMIT),
        name="sb_attention",
    )(slab3, slab3, slab3, slab3, slab3, tri, bias, slab3)


def _gla_kernel(q_ref, k_ref, v_ref, lr_ref, w2_ref, b2_ref, low_ref, o_ref, state_ref):
    c = GLA_CHUNK
    kw = GLA_KEY_WIDTH

    @pl.when(pl.program_id(1) == 0)
    def _():
        state_ref[...] = jnp.zeros_like(state_ref)

    heads = range(GLA_HEADS)
    chunks = range(GLA_CPS)
    ks = [slice(h * GLA_KEY_DIM, (h + 1) * GLA_KEY_DIM) for h in heads]
    vs = [slice(h * GLA_VAL_DIM, (h + 1) * GLA_VAL_DIM) for h in heads]
    rs = [slice(ci * c, (ci + 1) * c) for ci in chunks]

    x = jnp.dot(lr_ref[0], w2_ref[...], preferred_element_type=F32) + b2_ref[...]
    g = jnp.minimum(x, 0.0) - jnp.log(1.0 + jnp.exp2(jnp.abs(x) * (-LOG2E)))
    g_hi = g.astype(BF16)
    g_lo = (g - g_hi.astype(F32)).astype(BF16)
    bc2 = jnp.dot(low_ref[...], jnp.concatenate([g_hi, g_lo], axis=1), preferred_element_type=F32)
    bcum = bc2[:, :kw] + bc2[:, kw:]
    q_e = jnp.exp(bcum)
    kk = k_ref[0].astype(F32) * (1.0 / q_e)
    q_t = (q_ref[0].astype(F32) * q_e).astype(BF16)
    k_t = kk.astype(BF16)
    decay = [q_e[(ci + 1) * c - 1:(ci + 1) * c, :] for ci in chunks]
    k_h = [(kk[rs[ci], :] * decay[ci]).astype(BF16) for ci in chunks]

    row = lax.broadcasted_iota(jnp.int32, (c, c), 0)
    col = lax.broadcasted_iota(jnp.int32, (c, c), 1)
    causal = col <= row
    sc = [[lax.dot_general(q_t[rs[ci], ks[h]], k_t[rs[ci], ks[h]], _NT, preferred_element_type=F32)
           for h in heads] for ci in chunks]
    sc = [[jnp.where(causal, s, 0.0).astype(BF16) for s in per_chunk] for per_chunk in sc]
    st = [state_ref[h] for h in heads]
    o = []
    for ci in chunks:
        o.append([jnp.dot(sc[ci][h], v_ref[0, rs[ci], vs[h]], preferred_element_type=F32)
                  + lax.dot_general(q_t[rs[ci], ks[h]], st[h].astype(BF16), _NT,
                                    preferred_element_type=F32) for h in heads])
        st = [st[h] * decay[ci][:, ks[h]] + lax.dot_general(
            v_ref[0, rs[ci], vs[h]], k_h[ci][:, ks[h]], _TN, preferred_element_type=F32)
            for h in heads]
    for h in heads:
        state_ref[h] = st[h]
    for ci in chunks:
        for h in heads:
            o_ref[0, rs[ci], vs[h]] = o[ci][h].astype(BF16)


def _gla(slab3, w2_pad, b2, low):
    b, s, _ = slab3.shape
    c = GLA_CHUNK * GLA_CPS
    qb, kb = OFF_GLA_Q // GLA_KEY_WIDTH, OFF_GLA_K // GLA_KEY_WIDTH
    vb = OFF_GLA_V // GLA_WIDTH
    lb = OFF_GLA_LR // LANES
    return pl.pallas_call(
        _gla_kernel,
        grid=(b, s // c),
        in_specs=[
            pl.BlockSpec((1, c, GLA_KEY_WIDTH), lambda bi, i: (bi, i, qb)),
            pl.BlockSpec((1, c, GLA_KEY_WIDTH), lambda bi, i: (bi, i, kb)),
            pl.BlockSpec((1, c, GLA_WIDTH), lambda bi, i: (bi, i, vb)),
            pl.BlockSpec((1, c, LANES), lambda bi, i: (bi, i, lb)),
            pl.BlockSpec((LANES, GLA_KEY_WIDTH), lambda bi, i: (0, 0)),
            pl.BlockSpec((1, GLA_KEY_WIDTH), lambda bi, i: (0, 0)),
            pl.BlockSpec((c, c), lambda bi, i: (0, 0)),
        ],
        out_specs=pl.BlockSpec((1, c, GLA_WIDTH), lambda bi, i: (bi, i, 0)),
        out_shape=jax.ShapeDtypeStruct((b, s, GLA_WIDTH), BF16),
        scratch_shapes=[pltpu.VMEM((GLA_HEADS, GLA_VAL_DIM, GLA_KEY_DIM), F32)],
        compiler_params=pltpu.CompilerParams(
            dimension_semantics=("arbitrary", "arbitrary"),
            vmem_limit_bytes=VMEM_LIMIT),
        name="gla",
    )(slab3, slab3, slab3, slab3, w2_pad, b2, low)


def _headnorm_gate(o_ref, g_ref, w_ref, n_heads):
    width = o_ref.shape[1] // n_heads
    parts = []
    for h in range(n_heads):
        cs = slice(h * width, (h + 1) * width)
        o = o_ref[:, cs].astype(F32)
        ss = jnp.sum(o * o, axis=-1, keepdims=True)
        scale = w_ref[:, cs] * g_ref[:, cs].astype(F32)
        parts.append((o * lax.rsqrt(ss + width * NORM_EPS) * scale).astype(BF16))
    return jnp.concatenate(parts, axis=1)


def _outproj_kernel(sb_ref, gla_ref, sbg_ref, glag_ref, sbw_ref, glaw_ref, w_ref, x_ref, nw_ref,
                    o_ref):
    y = (jnp.dot(_headnorm_gate(sb_ref, sbg_ref, sbw_ref, SB_HEADS), w_ref[0:SB_WIDTH, :],
                 preferred_element_type=F32)
         + jnp.dot(_headnorm_gate(gla_ref, glag_ref, glaw_ref, GLA_HEADS), w_ref[SB_WIDTH:, :],
                   preferred_element_type=F32))
    ms = jnp.mean(y * y, axis=-1, keepdims=True)
    o_ref[...] = x_ref[...] + y * lax.rsqrt(ms + NORM_EPS) * nw_ref[...]


def _outproj(sb_o, gla_o, slab, sb_norm_w, gla_norm_w, w_out, x2, post_w):
    m = x2.shape[0]
    tm = OUTPROJ_TM
    return pl.pallas_call(
        _outproj_kernel,
        grid=(m // tm,),
        in_specs=[
            pl.BlockSpec((tm, SB_WIDTH), lambda i: (i, 0)),
            pl.BlockSpec((tm, GLA_WIDTH), lambda i: (i, 0)),
            pl.BlockSpec((tm, SB_WIDTH), lambda i: (i, OFF_SB_G // SB_WIDTH)),
            pl.BlockSpec((tm, GLA_WIDTH), lambda i: (i, OFF_GLA_G // GLA_WIDTH)),
            pl.BlockSpec((1, SB_WIDTH), lambda i: (0, 0)),
            pl.BlockSpec((1, GLA_WIDTH), lambda i: (0, 0)),
            pl.BlockSpec((SB_WIDTH + GLA_WIDTH, D_MODEL), lambda i: (0, 0)),
            pl.BlockSpec((tm, D_MODEL), lambda i: (i, 0)),
            pl.BlockSpec((1, D_MODEL), lambda i: (0, 0)),
        ],
        out_specs=pl.BlockSpec((tm, D_MODEL), lambda i: (i, 0)),
        out_shape=jax.ShapeDtypeStruct((m, D_MODEL), F32),
        compiler_params=pltpu.CompilerParams(
            dimension_semantics=("arbitrary",), vmem_limit_bytes=VMEM_LIMIT),
        name="outproj",
    )(sb_o, gla_o, slab, slab, sb_norm_w, gla_norm_w, w_out, x2, post_w)


def _layer(x, pre_norm_w, w_in, w_alpha2, b_alpha2, sb_norm_w, gla_norm_w, w_out, post_norm_w):
    b, s, d = x.shape
    m = b * s
    x2 = x.reshape(m, d)
    w_slab = jnp.pad(w_in, ((0, 0), (0, SLAB_WIDTH - IN_PROJ_WIDTH))).astype(BF16)
    slab = _inproj(x2, pre_norm_w.reshape(1, d), w_slab)
    slab3 = slab.reshape(b, s, SLAB_WIDTH)

    t = SB_TILE
    pos = jnp.arange(t)
    tri = -(pos[:, None] >= pos[None, :]).astype(BF16)
    bias = jnp.where(pos[None, :] < pos[:, None], 0.0, SB_MASKED_LOGIT).astype(F32)
    sb_o = _sb_attention(slab3, tri, bias)

    tok = jnp.arange(GLA_CHUNK * GLA_CPS)
    same_chunk = (tok[:, None] // GLA_CHUNK) == (tok[None, :] // GLA_CHUNK)
    low = jnp.where(same_chunk & (tok[:, None] >= tok[None, :]), 1.0 / GATE_TEMP, 0.0).astype(BF16)
    w2_pad = jnp.pad(w_alpha2, ((0, LANES - GATE_RANK), (0, 0))).astype(BF16)
    gla_o = _gla(slab3, w2_pad, b_alpha2.reshape(1, GLA_KEY_WIDTH), low)

    out = _outproj(sb_o.reshape(m, SB_WIDTH), gla_o.reshape(m, GLA_WIDTH), slab,
                   sb_norm_w.reshape(1, SB_WIDTH) * SB_HEAD_DIM ** 0.5,
                   gla_norm_w.reshape(1, GLA_WIDTH) * GLA_VAL_DIM ** 0.5,
                   w_out.astype(BF16), x2, post_norm_w.reshape(1, d))
    return out.reshape(b, s, d)


def kernel(x, pre_norm_w, w_in, w_alpha2, b_alpha2, sb_norm_w, gla_norm_w, w_out, post_norm_w):
    for layer in range(pre_norm_w.shape[0]):
        x = _layer(x, pre_norm_w[layer], w_in[layer], w_alpha2[layer], b_alpha2[layer],
                   sb_norm_w[layer], gla_norm_w[layer], w_out[layer], post_norm_w[layer])
    return x
```

```python
import math

import jax
import jax.numpy as jnp
from jax import lax
from jax.experimental import pallas as pl
from jax.experimental.pallas import tpu as pltpu

F32 = jnp.float32
BF16 = jnp.bfloat16

D_MODEL = 1024
SB_HEADS = 8
SB_HEAD_DIM = 128
SB_WIDTH = SB_HEADS * SB_HEAD_DIM
GLA_HEADS = 4
GLA_KEY_DIM = 128
GLA_VAL_DIM = 256
GLA_KEY_WIDTH = GLA_HEADS * GLA_KEY_DIM
GLA_WIDTH = GLA_HEADS * GLA_VAL_DIM
GATE_RANK = 16
GATE_TEMP = 16.0
NORM_EPS = 1e-6
LANES = 128

OFF_SB_Q = 0
OFF_SB_K = OFF_SB_Q + SB_WIDTH
OFF_SB_V = OFF_SB_K + SB_WIDTH
OFF_SB_G = OFF_SB_V + SB_WIDTH
OFF_GLA_Q = OFF_SB_G + SB_WIDTH
OFF_GLA_K = OFF_GLA_Q + GLA_KEY_WIDTH
OFF_GLA_V = OFF_GLA_K + GLA_KEY_WIDTH
OFF_GLA_G = OFF_GLA_V + GLA_WIDTH
OFF_GLA_LR = OFF_GLA_G + GLA_WIDTH
IN_PROJ_WIDTH = OFF_GLA_LR + GATE_RANK
SLAB_WIDTH = OFF_GLA_LR + LANES

LOG2E = math.log2(math.e)

INPROJ_TM = 512
INPROJ_TN = 512
SB_TILE = 256
SB_HPG = 8
SB_UNDERFLOW_LOG2 = -150.0
SB_MASKED_LOGIT = -1e30
SB_SOFTPLUS_CLAMP = 64.0
GLA_CHUNK = 128
GLA_CPS = 2
OUTPROJ_TM = 512
VMEM_LIMIT = 56 * 1024 * 1024

_NT = (((1,), (1,)), ((), ()))
_TN = (((0,), (0,)), ((), ()))


def _inproj_kernel(x_ref, nw_ref, w_ref, out_ref):
    x = x_ref[...]
    ms = jnp.mean(x * x, axis=-1, keepdims=True)
    h = (x * lax.rsqrt(ms + NORM_EPS) * nw_ref[...]).astype(BF16)
    for c0 in range(0, IN_PROJ_WIDTH, INPROJ_TN):
        c1 = min(c0 + INPROJ_TN, IN_PROJ_WIDTH)
        r = jnp.dot(h, w_ref[:, c0:c1], preferred_element_type=F32)
        if OFF_SB_Q <= c0 < OFF_SB_K:
            r = r * (LOG2E * SB_HEAD_DIM ** -0.5)
        elif OFF_GLA_Q <= c0 < OFF_GLA_K:
            r = r * (GLA_KEY_DIM ** -0.5)
        elif OFF_SB_G <= c0 < OFF_GLA_Q or OFF_GLA_G <= c0 < OFF_GLA_LR:
            r = r * (1.0 / (1.0 + jnp.exp2(r * (-LOG2E))))
        out_ref[:, c0:c1] = r.astype(BF16)
    out_ref[:, IN_PROJ_WIDTH:] = jnp.zeros((x.shape[0], SLAB_WIDTH - IN_PROJ_WIDTH), BF16)


def _inproj(x2, pre_w, w_in):
    m = x2.shape[0]
    return pl.pallas_call(
        _inproj_kernel,
        grid=(m // INPROJ_TM,),
        in_specs=[
            pl.BlockSpec((INPROJ_TM, D_MODEL), lambda i: (i, 0)),
            pl.BlockSpec((1, D_MODEL), lambda i: (0, 0)),
            pl.BlockSpec((D_MODEL, IN_PROJ_WIDTH), lambda i: (0, 0),
                         pipeline_mode=pl.Buffered(1)),
        ],
        out_specs=pl.BlockSpec((INPROJ_TM, SLAB_WIDTH), lambda i: (i, 0)),
        out_shape=jax.ShapeDtypeStruct((m, SLAB_WIDTH), BF16),
        compiler_params=pltpu.CompilerParams(
            dimension_semantics=("arbitrary",), vmem_limit_bytes=VMEM_LIMIT),
        name="inproj",
    )(x2, pre_w, w_in)


def _sb_logits(q, k, bias):
    z = lax.dot_general(q, k, _NT, preferred_element_type=F32)
    if bias is not None:
        z = z + bias
    sp = jnp.maximum(z, jnp.log2(1.0 + jnp.exp2(jnp.minimum(z, SB_SOFTPLUS_CLAMP))))
    return z, sp.astype(BF16)


def _sb_weights(z, sp, tri):
    c = jnp.dot(sp, tri, preferred_element_type=F32)
    return c[:, 0:1], jnp.exp2(z + c).astype(BF16)


def _sb_tile(q, k, v, tri, bias):
    z, sp = _sb_logits(q, k, bias)
    ls, p = _sb_weights(z, sp, tri)
    return ls, jnp.dot(p, v, preferred_element_type=F32)


def _sb_kernel(q_ref, kd_ref, vd_ref, kp_ref, vp_ref, tri_ref, bias_ref,
               slab_hbm, o_ref, acc_ref, ls_ref, kbuf, vbuf, sem):
    t = SB_TILE
    half = t // 2
    bi = pl.program_id(0)
    hg = pl.program_id(1)
    i = pl.program_id(2)
    prev_off = jnp.where(i > 0, 0.0, SB_MASKED_LOGIT).astype(F32)
    tri = tri_ref[...]
    tri_half = tri_ref[0:half, 0:half]

    def cols(h):
        return slice(h * SB_HEAD_DIM, (h + 1) * SB_HEAD_DIM)

    logits, weights = {}, {}
    for step in range(SB_HPG + 2):
        if step < SB_HPG:
            cs = cols(step)
            q = q_ref[0, :, cs]
            logits[step] = (
                _sb_logits(q[0:half], kd_ref[0, 0:half, cs], bias_ref[0:half, 0:half]),
                _sb_logits(q[half:t], kd_ref[0, :, cs], bias_ref[half:t, :]),
                _sb_logits(q, kp_ref[0, :, cs], None))
        if 0 <= step - 1 < SB_HPG:
            top, bot, prev = logits.pop(step - 1)
            weights[step - 1] = (_sb_weights(*top, tri_half), _sb_weights(*bot, tri),
                                 _sb_weights(*prev, tri))
        if 0 <= step - 2 < SB_HPG:
            h = step - 2
            cs = cols(h)
            (ls_t, p_t), (ls_b, p_b), (ls_p, p_p) = weights.pop(h)
            pv_t = jnp.dot(p_t, vd_ref[0, 0:half, cs], preferred_element_type=F32)
            pv_b = jnp.dot(p_b, vd_ref[0, :, cs], preferred_element_type=F32)
            pv_p = jnp.dot(p_p, vp_ref[0, :, cs], preferred_element_type=F32)
            acc_ref[h, 0:half] = pv_t + jnp.exp2(ls_t + prev_off) * pv_p[0:half]
            acc_ref[h, half:t] = pv_b + jnp.exp2(ls_b + prev_off) * pv_p[half:t]
            ls_ref[h, 0:half] = ls_t + ls_p[0:half]
            ls_ref[h, half:t] = ls_b + ls_p[half:t]

    more = jnp.logical_and(i >= 2, jnp.max(ls_ref[...]) > SB_UNDERFLOW_LOG2)

    @pl.when(more)
    def _():
        for h in range(SB_HPG):
            cs = cols(h)
            head = hg * SB_HPG + h

            def tile_copies(j, head=head):
                rows = pl.ds(pl.multiple_of(j * t, t), t)
                kcol = pl.ds(pl.multiple_of(OFF_SB_K + head * SB_HEAD_DIM, SB_HEAD_DIM), SB_HEAD_DIM)
                vcol = pl.ds(pl.multiple_of(OFF_SB_V + head * SB_HEAD_DIM, SB_HEAD_DIM), SB_HEAD_DIM)
                return (pltpu.make_async_copy(slab_hbm.at[bi, rows, kcol], kbuf, sem.at[0]),
                        pltpu.make_async_copy(slab_hbm.at[bi, rows, vcol], vbuf, sem.at[1]))

            def cond(carry):
                j, ls, _ = carry
                return jnp.logical_and(j >= 0, jnp.max(ls) > SB_UNDERFLOW_LOG2)

            def body(carry, cs=cs, tile_copies=tile_copies):
                j, ls, acc = carry
                kc, vc = tile_copies(j)
                kc.start()
                vc.start()
                kc.wait()
                vc.wait()
                ls_j, pv = _sb_tile(q_ref[0, :, cs], kbuf[...], vbuf[...], tri, None)
                return j - 1, ls + ls_j, acc + jnp.exp2(ls) * pv

            _, _, acc = lax.while_loop(cond, body, (i - 2, ls_ref[h], acc_ref[h]))
            acc_ref[h] = acc

    for h in range(SB_HPG):
        o_ref[0, :, cols(h)] = acc_ref[h].astype(BF16)


def _sb_attention(slab3, tri, bias):
    b, s, _ = slab3.shape
    t = SB_TILE
    gw = SB_HPG * SB_HEAD_DIM
    qb, kb, vb = OFF_SB_Q // gw, OFF_SB_K // gw, OFF_SB_V // gw

    def tile_spec(cb):
        return pl.BlockSpec((1, t, gw), lambda bi, hg, i: (bi, i, cb + hg))

    def prev_spec(cb):
        return pl.BlockSpec((1, t, gw), lambda bi, hg, i: (bi, jnp.maximum(i - 1, 0), cb + hg))

    return pl.pallas_call(
        _sb_kernel,
        grid=(b, SB_HEADS // SB_HPG, s // t),
        in_specs=[
            tile_spec(qb), tile_spec(kb), tile_spec(vb), prev_spec(kb), prev_spec(vb),
            pl.BlockSpec((t, t), lambda bi, hg, i: (0, 0)),
            pl.BlockSpec((t, t), lambda bi, hg, i: (0, 0)),
            pl.BlockSpec(memory_space=pl.ANY),
        ],
        out_specs=pl.BlockSpec((1, t, gw), lambda bi, hg, i: (bi, i, hg)),
        out_shape=jax.ShapeDtypeStruct((b, s, SB_WIDTH), BF16),
        scratch_shapes=[
            pltpu.VMEM((SB_HPG, t, SB_HEAD_DIM), F32),
            pltpu.VMEM((SB_HPG, t, 1), F32),
            pltpu.VMEM((t, SB_HEAD_DIM), BF16),
            pltpu.VMEM((t, SB_HEAD_DIM), BF16),
            pltpu.SemaphoreType.DMA((2,)),
        ],
        compiler_params=pltpu.CompilerParams(
            dimension_semantics=("arbitrary", "arbitrary", "arbitrary"),
            vmem_limit_bytes=VMEM_LIMIT),
        name="sb_attention",
    )(slab3, slab3, slab3, slab3, slab3, tri, bias, slab3)


def _gla_kernel(q_ref, k_ref, v_ref, lr_ref, w2_ref, b2_ref, low_ref, o_ref, state_ref):
    c = GLA_CHUNK
    kw = GLA_KEY_WIDTH

    @pl.when(pl.program_id(1) == 0)
    def _():
        state_ref[...] = jnp.zeros_like(state_ref)

    heads = range(GLA_HEADS)
    chunks = range(GLA_CPS)
    ks = [slice(h * GLA_KEY_DIM, (h + 1) * GLA_KEY_DIM) for h in heads]
    vs = [slice(h * GLA_VAL_DIM, (h + 1) * GLA_VAL_DIM) for h in heads]
    rs = [slice(ci * c, (ci + 1) * c) for ci in chunks]

    x = jnp.dot(lr_ref[0], w2_ref[...], preferred_element_type=F32) + b2_ref[...]
    g = jnp.minimum(x, 0.0) - jnp.log(1.0 + jnp.exp2(jnp.abs(x) * (-LOG2E)))
    g_hi = g.astype(BF16)
    g_lo = (g - g_hi.astype(F32)).astype(BF16)
    bc2 = jnp.dot(low_ref[...], jnp.concatenate([g_hi, g_lo], axis=1), preferred_element_type=F32)
    bcum = bc2[:, :kw] + bc2[:, kw:]
    q_e = jnp.exp(bcum)
    kk = k_ref[0].astype(F32) * (1.0 / q_e)
    q_t = (q_ref[0].astype(F32) * q_e).astype(BF16)
    k_t = kk.astype(BF16)
    decay = [q_e[(ci + 1) * c - 1:(ci + 1) * c, :] for ci in chunks]
    k_h = [(kk[rs[ci], :] * decay[ci]).astype(BF16) for ci in chunks]

    row = lax.broadcasted_iota(jnp.int32, (c, c), 0)
    col = lax.broadcasted_iota(jnp.int32, (c, c), 1)
    causal = col <= row
    sc = [[lax.dot_general(q_t[rs[ci], ks[h]], k_t[rs[ci], ks[h]], _NT, preferred_element_type=F32)
           for h in heads] for ci in chunks]
    sc = [[jnp.where(causal, s, 0.0).astype(BF16) for s in per_chunk] for per_chunk in sc]
    st = [state_ref[h] for h in heads]
    o = []
    for ci in chunks:
        o.append([jnp.dot(sc[ci][h], v_ref[0, rs[ci], vs[h]], preferred_element_type=F32)
                  + lax.dot_general(q_t[rs[ci], ks[h]], st[h].astype(BF16), _NT,
                                    preferred_element_type=F32) for h in heads])
        st = [st[h] * decay[ci][:, ks[h]] + lax.dot_general(
            v_ref[0, rs[ci], vs[h]], k_h[ci][:, ks[h]], _TN, preferred_element_type=F32)
            for h in heads]
    for h in heads:
        state_ref[h] = st[h]
    for ci in chunks:
        for h in heads:
            o_ref[0, rs[ci], vs[h]] = o[ci][h].astype(BF16)


def _gla(slab3, w2_pad, b2, low):
    b, s, _ = slab3.shape
    c = GLA_CHUNK * GLA_CPS
    qb, kb = OFF_GLA_Q // GLA_KEY_WIDTH, OFF_GLA_K // GLA_KEY_WIDTH
    vb = OFF_GLA_V // GLA_WIDTH
    lb = OFF_GLA_LR // LANES
    return pl.pallas_call(
        _gla_kernel,
        grid=(b, s // c),
        in_specs=[
            pl.BlockSpec((1, c, GLA_KEY_WIDTH), lambda bi, i: (bi, i, qb)),
            pl.BlockSpec((1, c, GLA_KEY_WIDTH), lambda bi, i: (bi, i, kb)),
            pl.BlockSpec((1, c, GLA_WIDTH), lambda bi, i: (bi, i, vb)),
            pl.BlockSpec((1, c, LANES), lambda bi, i: (bi, i, lb)),
            pl.BlockSpec((LANES, GLA_KEY_WIDTH), lambda bi, i: (0, 0)),
            pl.BlockSpec((1, GLA_KEY_WIDTH), lambda bi, i: (0, 0)),
            pl.BlockSpec((c, c), lambda bi, i: (0, 0)),
        ],
        out_specs=pl.BlockSpec((1, c, GLA_WIDTH), lambda bi, i: (bi, i, 0)),
        out_shape=jax.ShapeDtypeStruct((b, s, GLA_WIDTH), BF16),
        scratch_shapes=[pltpu.VMEM((GLA_HEADS, GLA_VAL_DIM, GLA_KEY_DIM), F32)],
        compiler_params=pltpu.CompilerParams(
            dimension_semantics=("arbitrary", "arbitrary"),
            vmem_limit_bytes=VMEM_LIMIT),
        name="gla",
    )(slab3, slab3, slab3, slab3, w2_pad, b2, low)


def _headnorm_gate(o_ref, g_ref, w_ref, n_heads):
    width = o_ref.shape[1] // n_heads
    parts = []
    for h in range(n_heads):
        cs = slice(h * width, (h + 1) * width)
        o = o_ref[:, cs].astype(F32)
        ss = jnp.sum(o * o, axis=-1, keepdims=True)
        scale = w_ref[:, cs] * g_ref[:, cs].astype(F32)
        parts.append((o * lax.rsqrt(ss + width * NORM_EPS) * scale).astype(BF16))
    return jnp.concatenate(parts, axis=1)


def _outproj_kernel(sb_ref, gla_ref, sbg_ref, glag_ref, sbw_ref, glaw_ref, w_ref, x_ref, nw_ref,
                    o_ref):
    y = (jnp.dot(_headnorm_gate(sb_ref, sbg_ref, sbw_ref, SB_HEADS), w_ref[0:SB_WIDTH, :],
                 preferred_element_type=F32)
         + jnp.dot(_headnorm_gate(gla_ref, glag_ref, glaw_ref, GLA_HEADS), w_ref[SB_WIDTH:, :],
                   preferred_element_type=F32))
    ms = jnp.mean(y * y, axis=-1, keepdims=True)
    o_ref[...] = x_ref[...] + y * lax.rsqrt(ms + NORM_EPS) * nw_ref[...]


def _outproj(sb_o, gla_o, slab, sb_norm_w, gla_norm_w, w_out, x2, post_w):
    m = x2.shape[0]
    tm = OUTPROJ_TM
    return pl.pallas_call(
        _outproj_kernel,
        grid=(m // tm,),
        in_specs=[
            pl.BlockSpec((tm, SB_WIDTH), lambda i: (i, 0)),
            pl.BlockSpec((tm, GLA_WIDTH), lambda i: (i, 0)),
            pl.BlockSpec((tm, SB_WIDTH), lambda i: (i, OFF_SB_G // SB_WIDTH)),
            pl.BlockSpec((tm, GLA_WIDTH), lambda i: (i, OFF_GLA_G // GLA_WIDTH)),
            pl.BlockSpec((1, SB_WIDTH), lambda i: (0, 0)),
            pl.BlockSpec((1, GLA_WIDTH), lambda i: (0, 0)),
            pl.BlockSpec((SB_WIDTH + GLA_WIDTH, D_MODEL), lambda i: (0, 0)),
            pl.BlockSpec((tm, D_MODEL), lambda i: (i, 0)),
            pl.BlockSpec((1, D_MODEL), lambda i: (0, 0)),
        ],
        out_specs=pl.BlockSpec((tm, D_MODEL), lambda i: (i, 0)),
        out_shape=jax.ShapeDtypeStruct((m, D_MODEL), F32),
        compiler_params=pltpu.CompilerParams(
            dimension_semantics=("arbitrary",), vmem_limit_bytes=VMEM_LIMIT),
        name="outproj",
    )(sb_o, gla_o, slab, slab, sb_norm_w, gla_norm_w, w_out, x2, post_w)


def _layer(x, pre_norm_w, w_in, w_alpha2, b_alpha2, sb_norm_w, gla_norm_w, w_out, post_norm_w):
    b, s, d = x.shape
    m = b * s
    x2 = x.reshape(m, d)
    slab = _inproj(x2, pre_norm_w.reshape(1, d), w_in.astype(BF16))
    slab3 = slab.reshape(b, s, SLAB_WIDTH)

    t = SB_TILE
    pos = jnp.arange(t)
    tri = -(pos[:, None] >= pos[None, :]).astype(BF16)
    bias = jnp.where(pos[None, :] < pos[:, None], 0.0, SB_MASKED_LOGIT).astype(F32)
    sb_o = _sb_attention(slab3, tri, bias)

    tok = jnp.arange(GLA_CHUNK * GLA_CPS)
    same_chunk = (tok[:, None] // GLA_CHUNK) == (tok[None, :] // GLA_CHUNK)
    low = jnp.where(same_chunk & (tok[:, None] >= tok[None, :]), 1.0 / GATE_TEMP, 0.0).astype(BF16)
    w2_pad = jnp.pad(w_alpha2, ((0, LANES - GATE_RANK), (0, 0))).astype(BF16)
    gla_o = _gla(slab3, w2_pad, b_alpha2.reshape(1, GLA_KEY_WIDTH), low)

    out = _outproj(sb_o.reshape(m, SB_WIDTH), gla_o.reshape(m, GLA_WIDTH), slab,
                   sb_norm_w.reshape(1, SB_WIDTH) * SB_HEAD_DIM ** 0.5,
                   gla_norm_w.reshape(1, GLA_WIDTH) * GLA_VAL_DIM ** 0.5,
                   w_out.astype(BF16), x2, post_norm_w.reshape(1, d))
    return out.reshape(b, s, d)


def kernel(x, pre_norm_w, w_in, w_alpha2, b_alpha2, sb_norm_w, gla_norm_w, w_out, post_norm_w):
    for layer in range(pre_norm_w.shape[0]):
        x = _layer(x, pre_norm_w[layer], w_in[layer], w_alpha2[layer], b_alpha2[layer],
                   sb_norm_w[layer], gla_norm_w[layer], w_out[layer], post_norm_w[layer])
    return x
```

```python
import math

import jax
import jax.numpy as jnp
from jax import lax
from jax.experimental import pallas as pl
from jax.experimental.pallas import tpu as pltpu

F32 = jnp.float32
BF16 = jnp.bfloat16

D_MODEL = 1024
SB_HEADS = 8
SB_HEAD_DIM = 128
SB_WIDTH = SB_HEADS * SB_HEAD_DIM
GLA_HEADS = 4
GLA_KEY_DIM = 128
GLA_VAL_DIM = 256
GLA_KEY_WIDTH = GLA_HEADS * GLA_KEY_DIM
GLA_WIDTH = GLA_HEADS * GLA_VAL_DIM
GATE_RANK = 16
GATE_TEMP = 16.0
NORM_EPS = 1e-6
LANES = 128
SUBLANES = 8

OFF_SB_Q = 0
OFF_SB_K = OFF_SB_Q + SB_WIDTH
OFF_SB_V = OFF_SB_K + SB_WIDTH
OFF_SB_G = OFF_SB_V + SB_WIDTH
OFF_GLA_Q = OFF_SB_G + SB_WIDTH
OFF_GLA_K = OFF_GLA_Q + GLA_KEY_WIDTH
OFF_GLA_V = OFF_GLA_K + GLA_KEY_WIDTH
OFF_GLA_G = OFF_GLA_V + GLA_WIDTH
OFF_GLA_LR = OFF_GLA_G + GLA_WIDTH
IN_PROJ_WIDTH = OFF_GLA_LR + GATE_RANK
SLAB_WIDTH = OFF_GLA_LR + LANES

LOG2E = math.log2(math.e)

INPROJ_TM = 512
INPROJ_TN = 512
SB_TILE = 256
SB_HPG = 8
SB_UNDERFLOW_LOG2 = -150.0
SB_MASKED_LOGIT = -1e30
SB_SOFTPLUS_CLAMP = 64.0
GLA_CHUNK = 128
GLA_CPS = 4
GLA_CUMSUM_CHUNKS = 2
GLA_SAFE_LOG_DECAY = -60.0
OUTPROJ_TM = 512
VMEM_LIMIT = 56 * 1024 * 1024

_NT = (((1,), (1,)), ((), ()))
_TN = (((0,), (0,)), ((), ()))


def _inproj_kernel(x_ref, nw_ref, w_ref, out_ref):
    x = x_ref[...]
    ms = jnp.mean(x * x, axis=-1, keepdims=True)
    h = (x * lax.rsqrt(ms + NORM_EPS) * nw_ref[...]).astype(BF16)
    for c0 in range(0, IN_PROJ_WIDTH, INPROJ_TN):
        c1 = min(c0 + INPROJ_TN, IN_PROJ_WIDTH)
        r = jnp.dot(h, w_ref[:, c0:c1], preferred_element_type=F32)
        if OFF_SB_Q <= c0 < OFF_SB_K:
            r = r * (LOG2E * SB_HEAD_DIM ** -0.5)
        elif OFF_GLA_Q <= c0 < OFF_GLA_K:
            r = r * (GLA_KEY_DIM ** -0.5)
        elif OFF_SB_G <= c0 < OFF_GLA_Q or OFF_GLA_G <= c0 < OFF_GLA_LR:
            r = r * (1.0 / (1.0 + jnp.exp2(r * (-LOG2E))))
        out_ref[:, c0:c1] = r.astype(BF16)
    out_ref[:, IN_PROJ_WIDTH:] = jnp.zeros((x.shape[0], SLAB_WIDTH - IN_PROJ_WIDTH), BF16)


def _inproj(x2, pre_w, w_in):
    m = x2.shape[0]
    return pl.pallas_call(
        _inproj_kernel,
        grid=(m // INPROJ_TM,),
        in_specs=[
            pl.BlockSpec((INPROJ_TM, D_MODEL), lambda i: (i, 0)),
            pl.BlockSpec((1, D_MODEL), lambda i: (0, 0)),
            pl.BlockSpec((D_MODEL, IN_PROJ_WIDTH), lambda i: (0, 0),
                         pipeline_mode=pl.Buffered(1)),
        ],
        out_specs=pl.BlockSpec((INPROJ_TM, SLAB_WIDTH), lambda i: (i, 0)),
        out_shape=jax.ShapeDtypeStruct((m, SLAB_WIDTH), BF16),
        compiler_params=pltpu.CompilerParams(
            dimension_semantics=("arbitrary",), vmem_limit_bytes=VMEM_LIMIT),
        name="inproj",
    )(x2, pre_w, w_in)


def _sb_logits(q, k, bias):
    z = lax.dot_general(q, k, _NT, preferred_element_type=F32)
    if bias is not None:
        z = z + bias
    sp = jnp.maximum(z, jnp.log2(1.0 + jnp.exp2(jnp.minimum(z, SB_SOFTPLUS_CLAMP))))
    return z, sp.astype(BF16)


def _sb_weights(z, sp, tri):
    c = jnp.dot(sp, tri, preferred_element_type=F32)
    return c[:, 0:1], jnp.exp2(z + c).astype(BF16)


def _sb_tile(q, k, v, tri, bias):
    z, sp = _sb_logits(q, k, bias)
    ls, p = _sb_weights(z, sp, tri)
    return ls, jnp.dot(p, v, preferred_element_type=F32)


def _sb_kernel(q_ref, kd_ref, vd_ref, kp_ref, vp_ref, tri_ref, bias_ref,
               slab_hbm, o_ref, acc_ref, ls_ref, kbuf, vbuf, sem):
    t = SB_TILE
    half = t // 2
    bi = pl.program_id(0)
    hg = pl.program_id(1)
    i = pl.program_id(2)
    prev_off = jnp.where(i > 0, 0.0, SB_MASKED_LOGIT).astype(F32)
    tri = tri_ref[...]
    tri_half = tri_ref[0:half, 0:half]

    def cols(h):
        return slice(h * SB_HEAD_DIM, (h + 1) * SB_HEAD_DIM)

    logits, weights = {}, {}
    for step in range(SB_HPG + 2):
        if step < SB_HPG:
            cs = cols(step)
            q = q_ref[0, :, cs]
            logits[step] = (
                _sb_logits(q[0:half], kd_ref[0, 0:half, cs], bias_ref[0:half, 0:half]),
                _sb_logits(q[half:t], kd_ref[0, :, cs], bias_ref[half:t, :]),
                _sb_logits(q, kp_ref[0, :, cs], None))
        if 0 <= step - 1 < SB_HPG:
            top, bot, prev = logits.pop(step - 1)
            weights[step - 1] = (_sb_weights(*top, tri_half), _sb_weights(*bot, tri),
                                 _sb_weights(*prev, tri))
        if 0 <= step - 2 < SB_HPG:
            h = step - 2
            cs = cols(h)
            (ls_t, p_t), (ls_b, p_b), (ls_p, p_p) = weights.pop(h)
            pv_t = jnp.dot(p_t, vd_ref[0, 0:half, cs], preferred_element_type=F32)
            pv_b = jnp.dot(p_b, vd_ref[0, :, cs], preferred_element_type=F32)
            pv_p = jnp.dot(p_p, vp_ref[0, :, cs], preferred_element_type=F32)
            acc_ref[h, 0:half] = pv_t + jnp.exp2(ls_t + prev_off) * pv_p[0:half]
            acc_ref[h, half:t] = pv_b + jnp.exp2(ls_b + prev_off) * pv_p[half:t]
            ls_ref[h, 0:half] = ls_t + ls_p[0:half]
            ls_ref[h, half:t] = ls_b + ls_p[half:t]

    more = jnp.logical_and(i >= 2, jnp.max(ls_ref[...]) > SB_UNDERFLOW_LOG2)

    @pl.when(more)
    def _():
        for h in range(SB_HPG):
            cs = cols(h)
            head = hg * SB_HPG + h

            def tile_copies(j, head=head):
                rows = pl.ds(pl.multiple_of(j * t, t), t)
                kcol = pl.ds(pl.multiple_of(OFF_SB_K + head * SB_HEAD_DIM, SB_HEAD_DIM), SB_HEAD_DIM)
                vcol = pl.ds(pl.multiple_of(OFF_SB_V + head * SB_HEAD_DIM, SB_HEAD_DIM), SB_HEAD_DIM)
                return (pltpu.make_async_copy(slab_hbm.at[bi, rows, kcol], kbuf, sem.at[0]),
                        pltpu.make_async_copy(slab_hbm.at[bi, rows, vcol], vbuf, sem.at[1]))

            def cond(carry):
                j, ls, _ = carry
                return jnp.logical_and(j >= 0, jnp.max(ls) > SB_UNDERFLOW_LOG2)

            def body(carry, cs=cs, tile_copies=tile_copies):
                j, ls, acc = carry
                kc, vc = tile_copies(j)
                kc.start()
                vc.start()
                kc.wait()
                vc.wait()
                ls_j, pv = _sb_tile(q_ref[0, :, cs], kbuf[...], vbuf[...], tri, None)
                return j - 1, ls + ls_j, acc + jnp.exp2(ls) * pv

            _, _, acc = lax.while_loop(cond, body, (i - 2, ls_ref[h], acc_ref[h]))
            acc_ref[h] = acc

    for h in range(SB_HPG):
        o_ref[0, :, cols(h)] = acc_ref[h].astype(BF16)


def _sb_attention(slab3, tri, bias):
    b, s, _ = slab3.shape
    t = SB_TILE
    gw = SB_HPG * SB_HEAD_DIM
    qb, kb, vb = OFF_SB_Q // gw, OFF_SB_K // gw, OFF_SB_V // gw

    def tile_spec(cb):
        return pl.BlockSpec((1, t, gw), lambda bi, hg, i: (bi, i, cb + hg))

    def prev_spec(cb):
        return pl.BlockSpec((1, t, gw), lambda bi, hg, i: (bi, jnp.maximum(i - 1, 0), cb + hg))

    return pl.pallas_call(
        _sb_kernel,
        grid=(b, SB_HEADS // SB_HPG, s // t),
        in_specs=[
            tile_spec(qb), tile_spec(kb), tile_spec(vb), prev_spec(kb), prev_spec(vb),
            pl.BlockSpec((t, t), lambda bi, hg, i: (0, 0)),
            pl.BlockSpec((t, t), lambda bi, hg, i: (0, 0)),
            pl.BlockSpec(memory_space=pl.ANY),
        ],
        out_specs=pl.BlockSpec((1, t, gw), lambda bi, hg, i: (bi, i, hg)),
        out_shape=jax.ShapeDtypeStruct((b, s, SB_WIDTH), BF16),
        scratch_shapes=[
            pltpu.VMEM((SB_HPG, t, SB_HEAD_DIM), F32),
            pltpu.VMEM((SB_HPG, t, 1), F32),
            pltpu.VMEM((t, SB_HEAD_DIM), BF16),
            pltpu.VMEM((t, SB_HEAD_DIM), BF16),
            pltpu.SemaphoreType.DMA((2,)),
        ],
        compiler_params=pltpu.CompilerParams(
            dimension_semantics=("arbitrary", "arbitrary", "arbitrary"),
            vmem_limit_bytes=VMEM_LIMIT),
        name="sb_attention",
    )(slab3, slab3, slab3, slab3, slab3, tri, bias, slab3)


def _gla_kernel(q_ref, k_ref, v_ref, lr_ref, w2_ref, b2_ref, low_ref, o_ref,
                state_ref, b_ref, kf_ref):
    c = GLA_CHUNK
    kw = GLA_KEY_WIDTH

    @pl.when(pl.program_id(1) == 0)
    def _():
        state_ref[...] = jnp.zeros_like(state_ref)

    heads = range(GLA_HEADS)
    chunks = range(GLA_CPS)
    ks = [slice(h * GLA_KEY_DIM, (h + 1) * GLA_KEY_DIM) for h in heads]
    vs = [slice(h * GLA_VAL_DIM, (h + 1) * GLA_VAL_DIM) for h in heads]
    rs = [slice(ci * c, (ci + 1) * c) for ci in chunks]

    x = jnp.dot(lr_ref[0], w2_ref[...], preferred_element_type=F32) + b2_ref[...]
    g = jnp.minimum(x, 0.0) - jnp.log(1.0 + jnp.exp2(jnp.abs(x) * (-LOG2E)))
    g_hi = g.astype(BF16)
    g_lo = (g - g_hi.astype(F32)).astype(BF16)
    g_split = jnp.concatenate([g_hi, g_lo], axis=1)
    span = GLA_CUMSUM_CHUNKS * c
    bc2 = jnp.concatenate(
        [jnp.dot(low_ref[...], g_split[r0:r0 + span, :], preferred_element_type=F32)
         for r0 in range(0, GLA_CPS * c, span)], axis=0)
    bcum = bc2[:, :kw] + bc2[:, kw:]
    q_e = jnp.exp(bcum)
    qf = q_ref[0].astype(F32)
    kf = k_ref[0].astype(F32)
    q_t = (qf * q_e).astype(BF16)
    b_last = [bcum[(ci + 1) * c - 1:(ci + 1) * c, :] for ci in chunks]
    decay = [q_e[(ci + 1) * c - 1:(ci + 1) * c, :] for ci in chunks]
    k_h = [(kf[rs[ci], :] * jnp.exp(b_last[ci] - bcum[rs[ci], :])).astype(BF16) for ci in chunks]

    row = lax.broadcasted_iota(jnp.int32, (c, c), 0)
    col = lax.broadcasted_iota(jnp.int32, (c, c), 1)
    causal = col <= row

    def finish(sc):
        st = [state_ref[h] for h in heads]
        o = []
        for ci in chunks:
            o.append([jnp.dot(sc[ci][h], v_ref[0, rs[ci], vs[h]], preferred_element_type=F32)
                      + lax.dot_general(q_t[rs[ci], ks[h]], st[h].astype(BF16), _NT,
                                        preferred_element_type=F32) for h in heads])
            st = [st[h] * decay[ci][:, ks[h]] + lax.dot_general(
                v_ref[0, rs[ci], vs[h]], k_h[ci][:, ks[h]], _TN, preferred_element_type=F32)
                for h in heads]
        for h in heads:
            state_ref[h] = st[h]
        for ci in chunks:
            for h in heads:
                o_ref[0, rs[ci], vs[h]] = o[ci][h].astype(BF16)

    factorise = jnp.min(g) > GLA_SAFE_LOG_DECAY * GATE_TEMP / c

    @pl.when(factorise)
    def _():
        k_t = (kf * (1.0 / q_e)).astype(BF16)
        sc = [[lax.dot_general(q_t[rs[ci], ks[h]], k_t[rs[ci], ks[h]], _NT,
                               preferred_element_type=F32) for h in heads] for ci in chunks]
        finish([[jnp.where(causal, s, 0.0).astype(BF16) for s in per_chunk] for per_chunk in sc])

    @pl.when(jnp.logical_not(factorise))
    def _():
        b_ref[...] = bcum
        kf_ref[...] = kf

        def exact_scores(ci, h):
            qh = qf[rs[ci], ks[h]]
            bh = bcum[rs[ci], ks[h]]

            def key_rows(grp, sc):
                base = pl.multiple_of(ci * c + grp * SUBLANES, SUBLANES)
                k_rows = kf_ref[pl.ds(base, SUBLANES), ks[h]]
                b_rows = b_ref[pl.ds(base, SUBLANES), ks[h]]
                for r in range(SUBLANES):
                    w = jnp.exp(jnp.minimum(bh - b_rows[r:r + 1, :], 0.0))
                    score_col = jnp.sum(qh * k_rows[r:r + 1, :] * w, axis=-1, keepdims=True)
                    sc = jnp.where(col == grp * SUBLANES + r, score_col, sc)
                return sc

            s = lax.fori_loop(0, c // SUBLANES, key_rows, jnp.zeros((c, c), F32))
            return jnp.where(causal, s, 0.0).astype(BF16)

        finish([[exact_scores(ci, h) for h in heads] for ci in chunks])


def _gla(slab3, w2_pad, b2, low):
    b, s, _ = slab3.shape
    c = GLA_CHUNK * GLA_CPS
    qb, kb = OFF_GLA_Q // GLA_KEY_WIDTH, OFF_GLA_K // GLA_KEY_WIDTH
    vb = OFF_GLA_V // GLA_WIDTH
    lb = OFF_GLA_LR // LANES
    return pl.pallas_call(
        _gla_kernel,
        grid=(b, s // c),
        in_specs=[
            pl.BlockSpec((1, c, GLA_KEY_WIDTH), lambda bi, i: (bi, i, qb)),
            pl.BlockSpec((1, c, GLA_KEY_WIDTH), lambda bi, i: (bi, i, kb)),
            pl.BlockSpec((1, c, GLA_WIDTH), lambda bi, i: (bi, i, vb)),
            pl.BlockSpec((1, c, LANES), lambda bi, i: (bi, i, lb)),
            pl.BlockSpec((LANES, GLA_KEY_WIDTH), lambda bi, i: (0, 0)),
            pl.BlockSpec((1, GLA_KEY_WIDTH), lambda bi, i: (0, 0)),
            pl.BlockSpec(low.shape, lambda bi, i: (0, 0)),
        ],
        out_specs=pl.BlockSpec((1, c, GLA_WIDTH), lambda bi, i: (bi, i, 0)),
        out_shape=jax.ShapeDtypeStruct((b, s, GLA_WIDTH), BF16),
        scratch_shapes=[
            pltpu.VMEM((GLA_HEADS, GLA_VAL_DIM, GLA_KEY_DIM), F32),
            pltpu.VMEM((c, GLA_KEY_WIDTH), F32),
            pltpu.VMEM((c, GLA_KEY_WIDTH), F32),
        ],
        compiler_params=pltpu.CompilerParams(
            dimension_semantics=("arbitrary", "arbitrary"),
            vmem_limit_bytes=VMEM_LIMIT),
        name="gla",
    )(slab3, slab3, slab3, slab3, w2_pad, b2, low)


def _headnorm_gate(o_ref, g_ref, w_ref, n_heads):
    width = o_ref.shape[1] // n_heads
    parts = []
    for h in range(n_heads):
        cs = slice(h * width, (h + 1) * width)
        o = o_ref[:, cs].astype(F32)
        ss = jnp.sum(o * o, axis=-1, keepdims=True)
        scale = w_ref[:, cs] * g_ref[:, cs].astype(F32)
        parts.append((o * lax.rsqrt(ss + width * NORM_EPS) * scale).astype(BF16))
    return jnp.concatenate(parts, axis=1)


def _outproj_kernel(sb_ref, gla_ref, sbg_ref, glag_ref, sbw_ref, glaw_ref, w_ref, x_ref, nw_ref,
                    o_ref):
    y = (jnp.dot(_headnorm_gate(sb_ref, sbg_ref, sbw_ref, SB_HEADS), w_ref[0:SB_WIDTH, :],
                 preferred_element_type=F32)
         + jnp.dot(_headnorm_gate(gla_ref, glag_ref, glaw_ref, GLA_HEADS), w_ref[SB_WIDTH:, :],
                   preferred_element_type=F32))
    ms = jnp.mean(y * y, axis=-1, keepdims=True)
    o_ref[...] = x_ref[...] + y * lax.rsqrt(ms + NORM_EPS) * nw_ref[...]


def _outproj(sb_o, gla_o, slab, sb_norm_w, gla_norm_w, w_out, x2, post_w):
    m = x2.shape[0]
    tm = OUTPROJ_TM
    return pl.pallas_call(
        _outproj_kernel,
        grid=(m // tm,),
        in_specs=[
            pl.BlockSpec((tm, SB_WIDTH), lambda i: (i, 0)),
            pl.BlockSpec((tm, GLA_WIDTH), lambda i: (i, 0)),
            pl.BlockSpec((tm, SB_WIDTH), lambda i: (i, OFF_SB_G // SB_WIDTH)),
            pl.BlockSpec((tm, GLA_WIDTH), lambda i: (i, OFF_GLA_G // GLA_WIDTH)),
            pl.BlockSpec((1, SB_WIDTH), lambda i: (0, 0)),
            pl.BlockSpec((1, GLA_WIDTH), lambda i: (0, 0)),
            pl.BlockSpec((SB_WIDTH + GLA_WIDTH, D_MODEL), lambda i: (0, 0)),
            pl.BlockSpec((tm, D_MODEL), lambda i: (i, 0)),
            pl.BlockSpec((1, D_MODEL), lambda i: (0, 0)),
        ],
        out_specs=pl.BlockSpec((tm, D_MODEL), lambda i: (i, 0)),
        out_shape=jax.ShapeDtypeStruct((m, D_MODEL), F32),
        compiler_params=pltpu.CompilerParams(
            dimension_semantics=("arbitrary",), vmem_limit_bytes=VMEM_LIMIT),
        name="outproj",
    )(sb_o, gla_o, slab, slab, sb_norm_w, gla_norm_w, w_out, x2, post_w)


def _layer(x, pre_norm_w, w_in, w_alpha2, b_alpha2, sb_norm_w, gla_norm_w, w_out, post_norm_w):
    b, s, d = x.shape
    m = b * s
    x2 = x.reshape(m, d)
    slab = _inproj(x2, pre_norm_w.reshape(1, d), w_in.astype(BF16))
    slab3 = slab.reshape(b, s, SLAB_WIDTH)

    t = SB_TILE
    pos = jnp.arange(t)
    tri = -(pos[:, None] >= pos[None, :]).astype(BF16)
    bias = jnp.where(pos[None, :] < pos[:, None], 0.0, SB_MASKED_LOGIT).astype(F32)
    sb_o = _sb_attention(slab3, tri, bias)

    tok = jnp.arange(GLA_CHUNK * GLA_CUMSUM_CHUNKS)
    same_chunk = (tok[:, None] // GLA_CHUNK) == (tok[None, :] // GLA_CHUNK)
    low = jnp.where(same_chunk & (tok[:, None] >= tok[None, :]), 1.0 / GATE_TEMP, 0.0).astype(BF16)
    w2_pad = jnp.pad(w_alpha2, ((0, LANES - GATE_RANK), (0, 0))).astype(BF16)
    gla_o = _gla(slab3, w2_pad, b_alpha2.reshape(1, GLA_KEY_WIDTH), low)

    out = _outproj(sb_o.reshape(m, SB_WIDTH), gla_o.reshape(m, GLA_WIDTH), slab,
                   sb_norm_w.reshape(1, SB_WIDTH) * SB_HEAD_DIM ** 0.5,
                   gla_norm_w.reshape(1, GLA_WIDTH) * GLA_VAL_DIM ** 0.5,
                   w_out.astype(BF16), x2, post_norm_w.reshape(1, d))
    return out.reshape(b, s, d)


def kernel(x, pre_norm_w, w_in, w_alpha2, b_alpha2, sb_norm_w, gla_norm_w, w_out, post_norm_w):
    for layer in range(pre_norm_w.shape[0]):
        x = _layer(x, pre_norm_w[layer], w_in[layer], w_alpha2[layer], b_alpha2[layer],
                   sb_norm_w[layer], gla_norm_w[layer], w_out[layer], post_norm_w[layer])
    return x
```

```python
import math

import jax
import jax.numpy as jnp
from jax import lax
from jax.experimental import pallas as pl
from jax.experimental.pallas import tpu as pltpu

F32 = jnp.float32
BF16 = jnp.bfloat16

D_MODEL = 1024
SB_HEADS = 8
SB_HEAD_DIM = 128
SB_WIDTH = SB_HEADS * SB_HEAD_DIM
GLA_HEADS = 4
GLA_KEY_DIM = 128
GLA_VAL_DIM = 256
GLA_KEY_WIDTH = GLA_HEADS * GLA_KEY_DIM
GLA_WIDTH = GLA_HEADS * GLA_VAL_DIM
GATE_RANK = 16
GATE_TEMP = 16.0
NORM_EPS = 1e-6
LANES = 128
SUBLANES = 8

OFF_SB_Q = 0
OFF_SB_K = OFF_SB_Q + SB_WIDTH
OFF_SB_V = OFF_SB_K + SB_WIDTH
OFF_SB_G = OFF_SB_V + SB_WIDTH
OFF_GLA_Q = OFF_SB_G + SB_WIDTH
OFF_GLA_K = OFF_GLA_Q + GLA_KEY_WIDTH
OFF_GLA_V = OFF_GLA_K + GLA_KEY_WIDTH
OFF_GLA_G = OFF_GLA_V + GLA_WIDTH
OFF_GLA_LR = OFF_GLA_G + GLA_WIDTH
IN_PROJ_WIDTH = OFF_GLA_LR + GATE_RANK
SLAB_WIDTH = OFF_GLA_LR + LANES

LOG2E = math.log2(math.e)

INPROJ_TN = 512
SB_TILE = 256
SB_UNDERFLOW_LOG2 = -150.0
SB_MASKED_LOGIT = -1e30
SB_SOFTPLUS_CLAMP = 64.0
GLA_CHUNK = 128
GLA_CPS = 4
GLA_CUMSUM_CHUNKS = 2
GLA_SAFE_LOG_DECAY = -60.0
OUTPROJ_TM = 512
VMEM_LIMIT = 56 * 1024 * 1024

_NT = (((1,), (1,)), ((), ()))
_TN = (((0,), (0,)), ((), ()))


def _project_chunk(h, w_ref, c0):
    c1 = min(c0 + INPROJ_TN, IN_PROJ_WIDTH)
    r = jnp.dot(h, w_ref[:, c0:c1], preferred_element_type=F32)
    if OFF_SB_Q <= c0 < OFF_SB_K:
        r = r * (LOG2E * SB_HEAD_DIM ** -0.5)
    elif OFF_GLA_Q <= c0 < OFF_GLA_K:
        r = r * (GLA_KEY_DIM ** -0.5)
    elif OFF_SB_G <= c0 < OFF_GLA_Q or OFF_GLA_G <= c0 < OFF_GLA_LR:
        r = r * (1.0 / (1.0 + jnp.exp2(r * (-LOG2E))))
    return c1, r.astype(BF16)


def _sb_logits(q, k, bias):
    z = lax.dot_general(q, k, _NT, preferred_element_type=F32)
    if bias is not None:
        z = z + bias
    sp = jnp.maximum(z, jnp.log2(1.0 + jnp.exp2(jnp.minimum(z, SB_SOFTPLUS_CLAMP))))
    return z, sp.astype(BF16)


def _sb_weights(z, sp, tri):
    c = jnp.dot(sp, tri, preferred_element_type=F32)
    return c[:, 0:1], jnp.exp2(z + c).astype(BF16)


def _sb_tile(q, k, v, tri, bias):
    z, sp = _sb_logits(q, k, bias)
    ls, p = _sb_weights(z, sp, tri)
    return ls, jnp.dot(p, v, preferred_element_type=F32)


def _inproj_sb_kernel(x_ref, nw_ref, w_ref, tri_ref, bias_ref, slab_hbm, o_ref,
                      stage_ref, acc_ref, ls_ref, kbuf, vbuf, sem, wsem):
    t = SB_TILE
    half = t // 2
    bi = pl.program_id(0)
    i = pl.program_id(1)
    nq = pl.num_programs(1)
    step = bi * nq + i
    last_step = pl.num_programs(0) * nq - 1
    slot = lax.rem(step, 2)
    prev_slot = 1 - slot

    def slab_write(step_idx):
        rows = pl.ds(pl.multiple_of(lax.rem(step_idx, nq) * t, t), t)
        s = lax.rem(step_idx, 2)
        return pltpu.make_async_copy(stage_ref.at[s], slab_hbm.at[lax.div(step_idx, nq), rows, :],
                                     wsem.at[s])

    @pl.when(step == 0)
    def _():
        stage_ref[1] = jnp.zeros(stage_ref.shape[1:], BF16)

    @pl.when(step >= 2)
    def _():
        slab_write(step - 2).wait()

    x = x_ref[0]
    ms = jnp.mean(x * x, axis=-1, keepdims=True)
    h = (x * lax.rsqrt(ms + NORM_EPS) * nw_ref[...]).astype(BF16)

    def project(c0):
        c1, r = _project_chunk(h, w_ref, c0)
        stage_ref[slot, :, c0:c1] = r

    chunk_starts = list(range(0, IN_PROJ_WIDTH, INPROJ_TN))
    for c0 in [c for c in chunk_starts if c < OFF_SB_G]:
        project(c0)
    later_chunks = iter([c for c in chunk_starts if c >= OFF_SB_G])
    stage_ref[slot, :, IN_PROJ_WIDTH:] = jnp.zeros((t, SLAB_WIDTH - IN_PROJ_WIDTH), BF16)

    prev_off = jnp.where(i > 0, 0.0, SB_MASKED_LOGIT).astype(F32)
    tri = tri_ref[...]
    tri_half = tri_ref[0:half, 0:half]

    def cols(h_idx, off):
        return slice(off + h_idx * SB_HEAD_DIM, off + (h_idx + 1) * SB_HEAD_DIM)

    logits, weights = {}, {}
    for stage in range(SB_HEADS + 2):
        if stage < SB_HEADS:
            q = stage_ref[slot, :, cols(stage, OFF_SB_Q)]
            kc = cols(stage, OFF_SB_K)
            logits[stage] = (
                _sb_logits(q[0:half], stage_ref[slot, 0:half, kc], bias_ref[0:half, 0:half]),
                _sb_logits(q[half:t], stage_ref[slot, :, kc], bias_ref[half:t, :]),
                _sb_logits(q, stage_ref[prev_slot, :, kc], None))
        if 0 <= stage - 1 < SB_HEADS:
            top, bot, prev = logits.pop(stage - 1)
            weights[stage - 1] = (_sb_weights(*top, tri_half), _sb_weights(*bot, tri),
                                  _sb_weights(*prev, tri))
        if 0 <= stage - 2 < SB_HEADS:
            hd = stage - 2
            vc = cols(hd, OFF_SB_V)
            (ls_t, p_t), (ls_b, p_b), (ls_p, p_p) = weights.pop(hd)
            pv_t = jnp.dot(p_t, stage_ref[slot, 0:half, vc], preferred_element_type=F32)
            pv_b = jnp.dot(p_b, stage_ref[slot, :, vc], preferred_element_type=F32)
            pv_p = jnp.dot(p_p, stage_ref[prev_slot, :, vc], preferred_element_type=F32)
            acc_ref[hd, 0:half] = pv_t + jnp.exp2(ls_t + prev_off) * pv_p[0:half]
            acc_ref[hd, half:t] = pv_b + jnp.exp2(ls_b + prev_off) * pv_p[half:t]
            ls_ref[hd, 0:half] = ls_t + ls_p[0:half]
            ls_ref[hd, half:t] = ls_b + ls_p[half:t]
        c0 = next(later_chunks, None)
        if c0 is not None:
            project(c0)
    for c0 in later_chunks:
        project(c0)

    more = jnp.logical_and(i >= 2, jnp.max(ls_ref[...]) > SB_UNDERFLOW_LOG2)

    @pl.when(more)
    def _():
        for hd in range(SB_HEADS):

            def tile_copies(j, hd=hd):
                rows = pl.ds(pl.multiple_of(j * t, t), t)
                return (pltpu.make_async_copy(slab_hbm.at[bi, rows, cols(hd, OFF_SB_K)], kbuf, sem.at[0]),
                        pltpu.make_async_copy(slab_hbm.at[bi, rows, cols(hd, OFF_SB_V)], vbuf, sem.at[1]))

            def cond(carry):
                j, ls, _ = carry
                return jnp.logical_and(j >= 0, jnp.max(ls) > SB_UNDERFLOW_LOG2)

            def body(carry, hd=hd, tile_copies=tile_copies):
                j, ls, acc = carry
                kc, vc = tile_copies(j)
                kc.start()
                vc.start()
                kc.wait()
                vc.wait()
                ls_j, pv = _sb_tile(stage_ref[slot, :, cols(hd, OFF_SB_Q)], kbuf[...], vbuf[...], tri, None)
                return j - 1, ls + ls_j, acc + jnp.exp2(ls) * pv

            _, _, acc = lax.while_loop(cond, body, (i - 2, ls_ref[hd], acc_ref[hd]))
            acc_ref[hd] = acc

    for hd in range(SB_HEADS):
        o_ref[0, :, cols(hd, 0)] = acc_ref[hd].astype(BF16)

    slab_write(step).start()

    @pl.when(step == last_step)
    def _():
        @pl.when(step >= 1)
        def _():
            slab_write(step - 1).wait()
        slab_write(step).wait()


def _inproj_sb(x, pre_w, w_in, tri, bias):
    b, s, d = x.shape
    t = SB_TILE
    return pl.pallas_call(
        _inproj_sb_kernel,
        grid=(b, s // t),
        in_specs=[
            pl.BlockSpec((1, t, d), lambda bi, i: (bi, i, 0)),
            pl.BlockSpec((1, d), lambda bi, i: (0, 0)),
            pl.BlockSpec((d, IN_PROJ_WIDTH), lambda bi, i: (0, 0), pipeline_mode=pl.Buffered(1)),
            pl.BlockSpec((t, t), lambda bi, i: (0, 0)),
            pl.BlockSpec((t, t), lambda bi, i: (0, 0)),
        ],
        out_specs=[
            pl.BlockSpec(memory_space=pl.ANY),
            pl.BlockSpec((1, t, SB_WIDTH), lambda bi, i: (bi, i, 0)),
        ],
        out_shape=[
            jax.ShapeDtypeStruct((b, s, SLAB_WIDTH), BF16),
            jax.ShapeDtypeStruct((b, s, SB_WIDTH), BF16),
        ],
        scratch_shapes=[
            pltpu.VMEM((2, t, SLAB_WIDTH), BF16),
            pltpu.VMEM((SB_HEADS, t, SB_HEAD_DIM), F32),
            pltpu.VMEM((SB_HEADS, t, 1), F32),
            pltpu.VMEM((t, SB_HEAD_DIM), BF16),
            pltpu.VMEM((t, SB_HEAD_DIM), BF16),
            pltpu.SemaphoreType.DMA((2,)),
            pltpu.SemaphoreType.DMA((2,)),
        ],
        compiler_params=pltpu.CompilerParams(
            dimension_semantics=("arbitrary", "arbitrary"), vmem_limit_bytes=VMEM_LIMIT),
        name="inproj_sb",
    )(x, pre_w, w_in, tri, bias)


def _gla_kernel(q_ref, k_ref, v_ref, lr_ref, w2_ref, b2_ref, low_ref, o_ref,
                state_ref, b_ref, kf_ref):
    c = GLA_CHUNK
    kw = GLA_KEY_WIDTH

    @pl.when(pl.program_id(1) == 0)
    def _():
        state_ref[...] = jnp.zeros_like(state_ref)

    heads = range(GLA_HEADS)
    chunks = range(GLA_CPS)
    ks = [slice(h * GLA_KEY_DIM, (h + 1) * GLA_KEY_DIM) for h in heads]
    vs = [slice(h * GLA_VAL_DIM, (h + 1) * GLA_VAL_DIM) for h in heads]
    rs = [slice(ci * c, (ci + 1) * c) for ci in chunks]

    x = jnp.dot(lr_ref[0], w2_ref[...], preferred_element_type=F32) + b2_ref[...]
    g = jnp.minimum(x, 0.0) - jnp.log(1.0 + jnp.exp2(jnp.abs(x) * (-LOG2E)))
    g_hi = g.astype(BF16)
    g_lo = (g - g_hi.astype(F32)).astype(BF16)
    g_split = jnp.concatenate([g_hi, g_lo], axis=1)
    span = GLA_CUMSUM_CHUNKS * c
    bc2 = jnp.concatenate(
        [jnp.dot(low_ref[...], g_split[r0:r0 + span, :], preferred_element_type=F32)
         for r0 in range(0, GLA_CPS * c, span)], axis=0)
    bcum = bc2[:, :kw] + bc2[:, kw:]
    q_e = jnp.exp(bcum)
    qf = q_ref[0].astype(F32)
    kf = k_ref[0].astype(F32)
    q_t = (qf * q_e).astype(BF16)
    b_last = [bcum[(ci + 1) * c - 1:(ci + 1) * c, :] for ci in chunks]
    decay = [q_e[(ci + 1) * c - 1:(ci + 1) * c, :] for ci in chunks]
    k_h = [(kf[rs[ci], :] * jnp.exp(b_last[ci] - bcum[rs[ci], :])).astype(BF16) for ci in chunks]

    row = lax.broadcasted_iota(jnp.int32, (c, c), 0)
    col = lax.broadcasted_iota(jnp.int32, (c, c), 1)
    causal = col <= row

    def finish(sc):
        st = [state_ref[h] for h in heads]
        o = []
        for ci in chunks:
            o.append([jnp.dot(sc[ci][h], v_ref[0, rs[ci], vs[h]], preferred_element_type=F32)
                      + lax.dot_general(q_t[rs[ci], ks[h]], st[h].astype(BF16), _NT,
                                        preferred_element_type=F32) for h in heads])
            st = [st[h] * decay[ci][:, ks[h]] + lax.dot_general(
                v_ref[0, rs[ci], vs[h]], k_h[ci][:, ks[h]], _TN, preferred_element_type=F32)
                for h in heads]
        for h in heads:
            state_ref[h] = st[h]
        for ci in chunks:
            for h in heads:
                o_ref[0, rs[ci], vs[h]] = o[ci][h].astype(BF16)

    factorise = jnp.min(g) > GLA_SAFE_LOG_DECAY * GATE_TEMP / c

    @pl.when(factorise)
    def _():
        k_t = (kf * (1.0 / q_e)).astype(BF16)
        sc = [[lax.dot_general(q_t[rs[ci], ks[h]], k_t[rs[ci], ks[h]], _NT,
                               preferred_element_type=F32) for h in heads] for ci in chunks]
        finish([[jnp.where(causal, s, 0.0).astype(BF16) for s in per_chunk] for per_chunk in sc])

    @pl.when(jnp.logical_not(factorise))
    def _():
        b_ref[...] = bcum
        kf_ref[...] = kf

        def exact_scores(ci, h):
            qh = qf[rs[ci], ks[h]]
            bh = bcum[rs[ci], ks[h]]

            def key_rows(grp, sc):
                base = pl.multiple_of(ci * c + grp * SUBLANES, SUBLANES)
                k_rows = kf_ref[pl.ds(base, SUBLANES), ks[h]]
                b_rows = b_ref[pl.ds(base, SUBLANES), ks[h]]
                for r in range(SUBLANES):
                    w = jnp.exp(jnp.minimum(bh - b_rows[r:r + 1, :], 0.0))
                    score_col = jnp.sum(qh * k_rows[r:r + 1, :] * w, axis=-1, keepdims=True)
                    sc = jnp.where(col == grp * SUBLANES + r, score_col, sc)
                return sc

            s = lax.fori_loop(0, c // SUBLANES, key_rows, jnp.zeros((c, c), F32))
            return jnp.where(causal, s, 0.0).astype(BF16)

        finish([[exact_scores(ci, h) for h in heads] for ci in chunks])


def _gla(slab3, w2_pad, b2, low):
    b, s, _ = slab3.shape
    c = GLA_CHUNK * GLA_CPS
    qb, kb = OFF_GLA_Q // GLA_KEY_WIDTH, OFF_GLA_K // GLA_KEY_WIDTH
    vb = OFF_GLA_V // GLA_WIDTH
    lb = OFF_GLA_LR // LANES
    return pl.pallas_call(
        _gla_kernel,
        grid=(b, s // c),
        in_specs=[
            pl.BlockSpec((1, c, GLA_KEY_WIDTH), lambda bi, i: (bi, i, qb)),
            pl.BlockSpec((1, c, GLA_KEY_WIDTH), lambda bi, i: (bi, i, kb)),
            pl.BlockSpec((1, c, GLA_WIDTH), lambda bi, i: (bi, i, vb)),
            pl.BlockSpec((1, c, LANES), lambda bi, i: (bi, i, lb)),
            pl.BlockSpec((LANES, GLA_KEY_WIDTH), lambda bi, i: (0, 0)),
            pl.BlockSpec((1, GLA_KEY_WIDTH), lambda bi, i: (0, 0)),
            pl.BlockSpec(low.shape, lambda bi, i: (0, 0)),
        ],
        out_specs=pl.BlockSpec((1, c, GLA_WIDTH), lambda bi, i: (bi, i, 0)),
        out_shape=jax.ShapeDtypeStruct((b, s, GLA_WIDTH), BF16),
        scratch_shapes=[
            pltpu.VMEM((GLA_HEADS, GLA_VAL_DIM, GLA_KEY_DIM), F32),
            pltpu.VMEM((c, GLA_KEY_WIDTH), F32),
            pltpu.VMEM((c, GLA_KEY_WIDTH), F32),
        ],
        compiler_params=pltpu.CompilerParams(
            dimension_semantics=("arbitrary", "arbitrary"),
            vmem_limit_bytes=VMEM_LIMIT),
        name="gla",
    )(slab3, slab3, slab3, slab3, w2_pad, b2, low)


def _headnorm_gate(o_ref, g_ref, w_ref, n_heads):
    width = o_ref.shape[1] // n_heads
    parts = []
    for h in range(n_heads):
        cs = slice(h * width, (h + 1) * width)
        o = o_ref[:, cs].astype(F32)
        ss = jnp.sum(o * o, axis=-1, keepdims=True)
        scale = w_ref[:, cs] * g_ref[:, cs].astype(F32)
        parts.append((o * lax.rsqrt(ss + width * NORM_EPS) * scale).astype(BF16))
    return jnp.concatenate(parts, axis=1)


def _outproj_kernel(sb_ref, gla_ref, sbg_ref, glag_ref, sbw_ref, glaw_ref, w_ref, x_ref, nw_ref,
                    o_ref):
    y = (jnp.dot(_headnorm_gate(sb_ref, sbg_ref, sbw_ref, SB_HEADS), w_ref[0:SB_WIDTH, :],
                 preferred_element_type=F32)
         + jnp.dot(_headnorm_gate(gla_ref, glag_ref, glaw_ref, GLA_HEADS), w_ref[SB_WIDTH:, :],
                   preferred_element_type=F32))
    ms = jnp.mean(y * y, axis=-1, keepdims=True)
    o_ref[...] = x_ref[...] + y * lax.rsqrt(ms + NORM_EPS) * nw_ref[...]


def _outproj(sb_o, gla_o, slab, sb_norm_w, gla_norm_w, w_out, x2, post_w):
    m = x2.shape[0]
    tm = OUTPROJ_TM
    return pl.pallas_call(
        _outproj_kernel,
        grid=(m // tm,),
        in_specs=[
            pl.BlockSpec((tm, SB_WIDTH), lambda i: (i, 0)),
            pl.BlockSpec((tm, GLA_WIDTH), lambda i: (i, 0)),
            pl.BlockSpec((tm, SB_WIDTH), lambda i: (i, OFF_SB_G // SB_WIDTH)),
            pl.BlockSpec((tm, GLA_WIDTH), lambda i: (i, OFF_GLA_G // GLA_WIDTH)),
            pl.BlockSpec((1, SB_WIDTH), lambda i: (0, 0)),
            pl.BlockSpec((1, GLA_WIDTH), lambda i: (0, 0)),
            pl.BlockSpec((SB_WIDTH + GLA_WIDTH, D_MODEL), lambda i: (0, 0)),
            pl.BlockSpec((tm, D_MODEL), lambda i: (i, 0)),
            pl.BlockSpec((1, D_MODEL), lambda i: (0, 0)),
        ],
        out_specs=pl.BlockSpec((tm, D_MODEL), lambda i: (i, 0)),
        out_shape=jax.ShapeDtypeStruct((m, D_MODEL), F32),
        compiler_params=pltpu.CompilerParams(
            dimension_semantics=("arbitrary",), vmem_limit_bytes=VMEM_LIMIT),
        name="outproj",
    )(sb_o, gla_o, slab, slab, sb_norm_w, gla_norm_w, w_out, x2, post_w)


def _layer(x, pre_norm_w, w_in, w_alpha2, b_alpha2, sb_norm_w, gla_norm_w, w_out, post_norm_w):
    b, s, d = x.shape
    m = b * s

    t = SB_TILE
    pos = jnp.arange(t)
    tri = -(pos[:, None] >= pos[None, :]).astype(BF16)
    bias = jnp.where(pos[None, :] < pos[:, None], 0.0, SB_MASKED_LOGIT).astype(F32)
    slab3, sb_o = _inproj_sb(x, pre_norm_w.reshape(1, d), w_in.astype(BF16), tri, bias)

    tok = jnp.arange(GLA_CHUNK * GLA_CUMSUM_CHUNKS)
    same_chunk = (tok[:, None] // GLA_CHUNK) == (tok[None, :] // GLA_CHUNK)
    low = jnp.where(same_chunk & (tok[:, None] >= tok[None, :]), 1.0 / GATE_TEMP, 0.0).astype(BF16)
    w2_pad = jnp.pad(w_alpha2, ((0, LANES - GATE_RANK), (0, 0))).astype(BF16)
    gla_o = _gla(slab3, w2_pad, b_alpha2.reshape(1, GLA_KEY_WIDTH), low)

    out = _outproj(sb_o.reshape(m, SB_WIDTH), gla_o.reshape(m, GLA_WIDTH),
                   slab3.reshape(m, SLAB_WIDTH),
                   sb_norm_w.reshape(1, SB_WIDTH) * SB_HEAD_DIM ** 0.5,
                   gla_norm_w.reshape(1, GLA_WIDTH) * GLA_VAL_DIM ** 0.5,
                   w_out.astype(BF16), x.reshape(m, d), post_norm_w.reshape(1, d))
    return out.reshape(b, s, d)


def kernel(x, pre_norm_w, w_in, w_alpha2, b_alpha2, sb_norm_w, gla_norm_w, w_out, post_norm_w):
    for layer in range(pre_norm_w.shape[0]):
        x = _layer(x, pre_norm_w[layer], w_in[layer], w_alpha2[layer], b_alpha2[layer],
                   sb_norm_w[layer], gla_norm_w[layer], w_out[layer], post_norm_w[layer])
    return x
```

```python
import math

import jax
import jax.numpy as jnp
from jax import lax
from jax.experimental import pallas as pl
from jax.experimental.pallas import tpu as pltpu

F32 = jnp.float32
BF16 = jnp.bfloat16

D_MODEL = 1024
SB_HEADS = 8
SB_HEAD_DIM = 128
SB_WIDTH = SB_HEADS * SB_HEAD_DIM
GLA_HEADS = 4
GLA_KEY_DIM = 128
GLA_VAL_DIM = 256
GLA_KEY_WIDTH = GLA_HEADS * GLA_KEY_DIM
GLA_WIDTH = GLA_HEADS * GLA_VAL_DIM
GATE_RANK = 16
GATE_TEMP = 16.0
NORM_EPS = 1e-6
LANES = 128
SUBLANES = 8

OFF_SB_Q = 0
OFF_SB_K = OFF_SB_Q + SB_WIDTH
OFF_SB_V = OFF_SB_K + SB_WIDTH
OFF_SB_G = OFF_SB_V + SB_WIDTH
OFF_GLA_Q = OFF_SB_G + SB_WIDTH
OFF_GLA_K = OFF_GLA_Q + GLA_KEY_WIDTH
OFF_GLA_V = OFF_GLA_K + GLA_KEY_WIDTH
OFF_GLA_G = OFF_GLA_V + GLA_WIDTH
OFF_GLA_LR = OFF_GLA_G + GLA_WIDTH
IN_PROJ_WIDTH = OFF_GLA_LR + GATE_RANK
SLAB_WIDTH = OFF_GLA_LR + LANES

LOG2E = math.log2(math.e)

INPROJ_TN = 512
SB_TILE = 256
SB_UNDERFLOW_LOG2 = -150.0
SB_MASKED_LOGIT = -1e30
SB_SOFTPLUS_CLAMP = 64.0
GLA_CHUNK = 128
GLA_CPS = 4
GLA_CUMSUM_CHUNKS = 2
GLA_SAFE_LOG_DECAY = -60.0
VMEM_LIMIT = 56 * 1024 * 1024

_NT = (((1,), (1,)), ((), ()))
_TN = (((0,), (0,)), ((), ()))


def _project_chunk(h, w_ref, c0):
    c1 = min(c0 + INPROJ_TN, IN_PROJ_WIDTH)
    r = jnp.dot(h, w_ref[:, c0:c1], preferred_element_type=F32)
    if OFF_SB_Q <= c0 < OFF_SB_K:
        r = r * (LOG2E * SB_HEAD_DIM ** -0.5)
    elif OFF_GLA_Q <= c0 < OFF_GLA_K:
        r = r * (GLA_KEY_DIM ** -0.5)
    elif OFF_SB_G <= c0 < OFF_GLA_Q or OFF_GLA_G <= c0 < OFF_GLA_LR:
        r = r * (1.0 / (1.0 + jnp.exp2(r * (-LOG2E))))
    return c1, r.astype(BF16)


def _sb_logits(q, k, bias):
    z = lax.dot_general(q, k, _NT, preferred_element_type=F32)
    if bias is not None:
        z = z + bias
    sp = jnp.maximum(z, jnp.log2(1.0 + jnp.exp2(jnp.minimum(z, SB_SOFTPLUS_CLAMP))))
    return z, sp.astype(BF16)


def _sb_weights(z, sp, tri):
    c = jnp.dot(sp, tri, preferred_element_type=F32)
    return c[:, 0:1], jnp.exp2(z + c).astype(BF16)


def _sb_tile(q, k, v, tri, bias):
    z, sp = _sb_logits(q, k, bias)
    ls, p = _sb_weights(z, sp, tri)
    return ls, jnp.dot(p, v, preferred_element_type=F32)


def _inproj_sb_kernel(x_ref, nw_ref, w_ref, tri_ref, bias_ref, slab_hbm, o_ref,
                      stage_ref, acc_ref, ls_ref, kbuf, vbuf, sem, wsem):
    t = SB_TILE
    half = t // 2
    bi = pl.program_id(0)
    i = pl.program_id(1)
    nq = pl.num_programs(1)
    step = bi * nq + i
    last_step = pl.num_programs(0) * nq - 1
    slot = lax.rem(step, 2)
    prev_slot = 1 - slot

    def slab_write(step_idx):
        rows = pl.ds(pl.multiple_of(lax.rem(step_idx, nq) * t, t), t)
        s = lax.rem(step_idx, 2)
        return pltpu.make_async_copy(stage_ref.at[s], slab_hbm.at[lax.div(step_idx, nq), rows, :],
                                     wsem.at[s])

    @pl.when(step == 0)
    def _():
        stage_ref[1] = jnp.zeros(stage_ref.shape[1:], BF16)

    @pl.when(step >= 2)
    def _():
        slab_write(step - 2).wait()

    x = x_ref[0]
    ms = jnp.mean(x * x, axis=-1, keepdims=True)
    h = (x * lax.rsqrt(ms + NORM_EPS) * nw_ref[...]).astype(BF16)

    def project(c0):
        c1, r = _project_chunk(h, w_ref, c0)
        stage_ref[slot, :, c0:c1] = r

    chunk_starts = list(range(0, IN_PROJ_WIDTH, INPROJ_TN))
    for c0 in [c for c in chunk_starts if c < OFF_SB_G]:
        project(c0)
    later_chunks = iter([c for c in chunk_starts if c >= OFF_SB_G])
    stage_ref[slot, :, IN_PROJ_WIDTH:] = jnp.zeros((t, SLAB_WIDTH - IN_PROJ_WIDTH), BF16)

    prev_off = jnp.where(i > 0, 0.0, SB_MASKED_LOGIT).astype(F32)
    tri = tri_ref[...]
    tri_half = tri_ref[0:half, 0:half]

    def cols(h_idx, off):
        return slice(off + h_idx * SB_HEAD_DIM, off + (h_idx + 1) * SB_HEAD_DIM)

    logits, weights = {}, {}
    for stage in range(SB_HEADS + 2):
        if stage < SB_HEADS:
            q = stage_ref[slot, :, cols(stage, OFF_SB_Q)]
            kc = cols(stage, OFF_SB_K)
            logits[stage] = (
                _sb_logits(q[0:half], stage_ref[slot, 0:half, kc], bias_ref[0:half, 0:half]),
                _sb_logits(q[half:t], stage_ref[slot, :, kc], bias_ref[half:t, :]),
                _sb_logits(q, stage_ref[prev_slot, :, kc], None))
        if 0 <= stage - 1 < SB_HEADS:
            top, bot, prev = logits.pop(stage - 1)
            weights[stage - 1] = (_sb_weights(*top, tri_half), _sb_weights(*bot, tri),
                                  _sb_weights(*prev, tri))
        if 0 <= stage - 2 < SB_HEADS:
            hd = stage - 2
            vc = cols(hd, OFF_SB_V)
            (ls_t, p_t), (ls_b, p_b), (ls_p, p_p) = weights.pop(hd)
            pv_t = jnp.dot(p_t, stage_ref[slot, 0:half, vc], preferred_element_type=F32)
            pv_b = jnp.dot(p_b, stage_ref[slot, :, vc], preferred_element_type=F32)
            pv_p = jnp.dot(p_p, stage_ref[prev_slot, :, vc], preferred_element_type=F32)
            acc_ref[hd, 0:half] = pv_t + jnp.exp2(ls_t + prev_off) * pv_p[0:half]
            acc_ref[hd, half:t] = pv_b + jnp.exp2(ls_b + prev_off) * pv_p[half:t]
            ls_ref[hd, 0:half] = ls_t + ls_p[0:half]
            ls_ref[hd, half:t] = ls_b + ls_p[half:t]
        c0 = next(later_chunks, None)
        if c0 is not None:
            project(c0)
    for c0 in later_chunks:
        project(c0)

    more = jnp.logical_and(i >= 2, jnp.max(ls_ref[...]) > SB_UNDERFLOW_LOG2)

    @pl.when(more)
    def _():
        for hd in range(SB_HEADS):

            def tile_copies(j, hd=hd):
                rows = pl.ds(pl.multiple_of(j * t, t), t)
                return (pltpu.make_async_copy(slab_hbm.at[bi, rows, cols(hd, OFF_SB_K)], kbuf, sem.at[0]),
                        pltpu.make_async_copy(slab_hbm.at[bi, rows, cols(hd, OFF_SB_V)], vbuf, sem.at[1]))

            def cond(carry):
                j, ls, _ = carry
                return jnp.logical_and(j >= 0, jnp.max(ls) > SB_UNDERFLOW_LOG2)

            def body(carry, hd=hd, tile_copies=tile_copies):
                j, ls, acc = carry
                kc, vc = tile_copies(j)
                kc.start()
                vc.start()
                kc.wait()
                vc.wait()
                ls_j, pv = _sb_tile(stage_ref[slot, :, cols(hd, OFF_SB_Q)], kbuf[...], vbuf[...], tri, None)
                return j - 1, ls + ls_j, acc + jnp.exp2(ls) * pv

            _, _, acc = lax.while_loop(cond, body, (i - 2, ls_ref[hd], acc_ref[hd]))
            acc_ref[hd] = acc

    for hd in range(SB_HEADS):
        o_ref[0, :, cols(hd, 0)] = acc_ref[hd].astype(BF16)

    slab_write(step).start()

    @pl.when(step == last_step)
    def _():
        @pl.when(step >= 1)
        def _():
            slab_write(step - 1).wait()
        slab_write(step).wait()


def _inproj_sb(x, pre_w, w_in, tri, bias):
    b, s, d = x.shape
    t = SB_TILE
    return pl.pallas_call(
        _inproj_sb_kernel,
        grid=(b, s // t),
        in_specs=[
            pl.BlockSpec((1, t, d), lambda bi, i: (bi, i, 0)),
            pl.BlockSpec((1, d), lambda bi, i: (0, 0)),
            pl.BlockSpec((d, IN_PROJ_WIDTH), lambda bi, i: (0, 0), pipeline_mode=pl.Buffered(1)),
            pl.BlockSpec((t, t), lambda bi, i: (0, 0)),
            pl.BlockSpec((t, t), lambda bi, i: (0, 0)),
        ],
        out_specs=[
            pl.BlockSpec(memory_space=pl.ANY),
            pl.BlockSpec((1, t, SB_WIDTH), lambda bi, i: (bi, i, 0)),
        ],
        out_shape=[
            jax.ShapeDtypeStruct((b, s, SLAB_WIDTH), BF16),
            jax.ShapeDtypeStruct((b, s, SB_WIDTH), BF16),
        ],
        scratch_shapes=[
            pltpu.VMEM((2, t, SLAB_WIDTH), BF16),
            pltpu.VMEM((SB_HEADS, t, SB_HEAD_DIM), F32),
            pltpu.VMEM((SB_HEADS, t, 1), F32),
            pltpu.VMEM((t, SB_HEAD_DIM), BF16),
            pltpu.VMEM((t, SB_HEAD_DIM), BF16),
            pltpu.SemaphoreType.DMA((2,)),
            pltpu.SemaphoreType.DMA((2,)),
        ],
        compiler_params=pltpu.CompilerParams(
            dimension_semantics=("arbitrary", "arbitrary"), vmem_limit_bytes=VMEM_LIMIT),
        name="inproj_sb",
    )(x, pre_w, w_in, tri, bias)


def _gla_gate_logits(lr, w2_ref, b2_ref):
    x = jnp.dot(lr, w2_ref[...], preferred_element_type=F32) + b2_ref[...]
    return x, jnp.min(x) > GLA_SAFE_LOG_DECAY * GATE_TEMP / GLA_CHUNK + math.log(2.0)


def _gla_decay(x, q, k, low_ref):
    c = GLA_CHUNK
    kw = GLA_KEY_WIDTH
    chunks = range(x.shape[0] // c)
    g = jnp.minimum(x, 0.0) - jnp.log(1.0 + jnp.exp2(jnp.abs(x) * (-LOG2E)))
    g_hi = g.astype(BF16)
    g_lo = (g - g_hi.astype(F32)).astype(BF16)
    g_split = jnp.concatenate([g_hi, g_lo], axis=1)
    span = GLA_CUMSUM_CHUNKS * c
    bc2 = jnp.concatenate(
        [jnp.dot(low_ref[...], g_split[r0:r0 + span, :], preferred_element_type=F32)
         for r0 in range(0, g.shape[0], span)], axis=0)
    bcum = bc2[:, :kw] + bc2[:, kw:]
    q_e = jnp.exp(bcum)
    qf = q.astype(F32)
    kf = k.astype(F32)
    b_last = [bcum[(ci + 1) * c - 1:(ci + 1) * c, :] for ci in chunks]
    return dict(
        bcum=bcum, q_e=q_e, qf=qf, kf=kf,
        q_t=(qf * q_e).astype(BF16),
        decay=[q_e[(ci + 1) * c - 1:(ci + 1) * c, :] for ci in chunks],
        k_h=[(kf[ci * c:(ci + 1) * c, :] * jnp.exp(b_last[ci] - bcum[ci * c:(ci + 1) * c, :])
              ).astype(BF16) for ci in chunks])


def _gla_finish(ctx, exact, v_of, state_ref, b_ref, kf_ref, store_o):
    c = GLA_CHUNK
    heads = range(GLA_HEADS)
    chunks = range(len(ctx["decay"]))
    ks = [slice(h * GLA_KEY_DIM, (h + 1) * GLA_KEY_DIM) for h in heads]
    rs = [slice(ci * c, (ci + 1) * c) for ci in chunks]
    q_t, k_h, decay = ctx["q_t"], ctx["k_h"], ctx["decay"]
    row = lax.broadcasted_iota(jnp.int32, (c, c), 0)
    col = lax.broadcasted_iota(jnp.int32, (c, c), 1)
    causal = col <= row

    if not exact:
        k_t = (ctx["kf"] * (1.0 / ctx["q_e"])).astype(BF16)
        sc = [[lax.dot_general(q_t[rs[ci], ks[h]], k_t[rs[ci], ks[h]], _NT,
                               preferred_element_type=F32) for h in heads] for ci in chunks]
        sc = [[jnp.where(causal, s, 0.0).astype(BF16) for s in per_chunk] for per_chunk in sc]
    else:
        b_ref[...] = ctx["bcum"]
        kf_ref[...] = ctx["kf"]

        def exact_scores(ci, h):
            qh = ctx["qf"][rs[ci], ks[h]]
            bh = ctx["bcum"][rs[ci], ks[h]]

            def key_rows(grp, sc):
                base = pl.multiple_of(ci * c + grp * SUBLANES, SUBLANES)
                k_rows = kf_ref[pl.ds(base, SUBLANES), ks[h]]
                b_rows = b_ref[pl.ds(base, SUBLANES), ks[h]]
                for r in range(SUBLANES):
                    w = jnp.exp(jnp.minimum(bh - b_rows[r:r + 1, :], 0.0))
                    score_col = jnp.sum(qh * k_rows[r:r + 1, :] * w, axis=-1, keepdims=True)
                    sc = jnp.where(col == grp * SUBLANES + r, score_col, sc)
                return sc

            s = lax.fori_loop(0, c // SUBLANES, key_rows, jnp.zeros((c, c), F32))
            return jnp.where(causal, s, 0.0).astype(BF16)

        sc = [[exact_scores(ci, h) for h in heads] for ci in chunks]

    st = [state_ref[h] for h in heads]
    o = []
    for ci in chunks:
        o.append([jnp.dot(sc[ci][h], v_of(ci, h), preferred_element_type=F32)
                  + lax.dot_general(q_t[rs[ci], ks[h]], st[h].astype(BF16), _NT,
                                    preferred_element_type=F32) for h in heads])
        st = [st[h] * decay[ci][:, ks[h]] + lax.dot_general(
            v_of(ci, h), k_h[ci][:, ks[h]], _TN, preferred_element_type=F32) for h in heads]
    for h in heads:
        state_ref[h] = st[h]
    for ci in chunks:
        for h in heads:
            store_o(ci, h, o[ci][h])


def _headnorm_gate(o_ref, g_ref, w_ref, n_heads):
    width = o_ref.shape[1] // n_heads
    parts = []
    for h in range(n_heads):
        cs = slice(h * width, (h + 1) * width)
        o = o_ref[:, cs].astype(F32)
        ss = jnp.sum(o * o, axis=-1, keepdims=True)
        scale = w_ref[:, cs] * g_ref[:, cs].astype(F32)
        parts.append((o * lax.rsqrt(ss + width * NORM_EPS) * scale).astype(BF16))
    return jnp.concatenate(parts, axis=1)


def _gla_out_kernel(q_ref, k_ref, v_ref, lr_ref, w2_ref, b2_ref, low_ref,
                    sbo_ref, sbg_ref, glag_ref, sbw_ref, glaw_ref, wout_ref, x_ref, nw_ref,
                    o_ref, state_ref, b_ref, kf_ref, ysb_ref, glao_ref):
    c = GLA_CHUNK

    @pl.when(pl.program_id(1) == 0)
    def _():
        state_ref[...] = jnp.zeros_like(state_ref)

    x_gate, factorise = _gla_gate_logits(lr_ref[...], w2_ref, b2_ref)

    def gla_v(ci, hd):
        return v_ref[ci * c:(ci + 1) * c, hd * GLA_VAL_DIM:(hd + 1) * GLA_VAL_DIM]

    def store_gla_o(ci, hd, val):
        glao_ref[ci * c:(ci + 1) * c, hd * GLA_VAL_DIM:(hd + 1) * GLA_VAL_DIM] = val

    for exact in (False, True):
        @pl.when(factorise != exact)
        def _(exact=exact):
            gla = _gla_decay(x_gate, q_ref[...], k_ref[...], low_ref)
            ysb_ref[...] = jnp.dot(_headnorm_gate(sbo_ref, sbg_ref, sbw_ref, SB_HEADS),
                                   wout_ref[0:SB_WIDTH, :], preferred_element_type=F32)
            _gla_finish(gla, exact, gla_v, state_ref, b_ref, kf_ref, store_gla_o)
            gla_y = _headnorm_gate(glao_ref, glag_ref, glaw_ref, GLA_HEADS)
            half = gla_y.shape[0] // 2
            for r0 in (0, half):
                rr = slice(r0, r0 + half)
                y = ysb_ref[rr, :] + jnp.dot(gla_y[rr, :], wout_ref[SB_WIDTH:, :],
                                             preferred_element_type=F32)
                ms = jnp.mean(y * y, axis=-1, keepdims=True)
                o_ref[rr, :] = x_ref[rr, :] + y * lax.rsqrt(ms + NORM_EPS) * nw_ref[...]


def _gla_out(slab, sb_o, w2_pad, b2, low, sb_norm_w, gla_norm_w, w_out, x2, post_w, seq):
    m = x2.shape[0]
    c = GLA_CHUNK * GLA_CPS
    steps = seq // c

    def rows(col_block):
        return lambda bi, i: (bi * steps + i, col_block)

    const = lambda bi, i: (0, 0)
    return pl.pallas_call(
        _gla_out_kernel,
        grid=(m // seq, steps),
        in_specs=[
            pl.BlockSpec((c, GLA_KEY_WIDTH), rows(OFF_GLA_Q // GLA_KEY_WIDTH)),
            pl.BlockSpec((c, GLA_KEY_WIDTH), rows(OFF_GLA_K // GLA_KEY_WIDTH)),
            pl.BlockSpec((c, GLA_WIDTH), rows(OFF_GLA_V // GLA_WIDTH)),
            pl.BlockSpec((c, LANES), rows(OFF_GLA_LR // LANES)),
            pl.BlockSpec((LANES, GLA_KEY_WIDTH), const),
            pl.BlockSpec((1, GLA_KEY_WIDTH), const),
            pl.BlockSpec(low.shape, const),
            pl.BlockSpec((c, SB_WIDTH), rows(0)),
            pl.BlockSpec((c, SB_WIDTH), rows(OFF_SB_G // SB_WIDTH)),
            pl.BlockSpec((c, GLA_WIDTH), rows(OFF_GLA_G // GLA_WIDTH)),
            pl.BlockSpec((1, SB_WIDTH), const),
            pl.BlockSpec((1, GLA_WIDTH), const),
            pl.BlockSpec((SB_WIDTH + GLA_WIDTH, D_MODEL), const),
            pl.BlockSpec((c, D_MODEL), rows(0)),
            pl.BlockSpec((1, D_MODEL), const),
        ],
        out_specs=pl.BlockSpec((c, D_MODEL), rows(0)),
        out_shape=jax.ShapeDtypeStruct((m, D_MODEL), F32),
        scratch_shapes=[
            pltpu.VMEM((GLA_HEADS, GLA_VAL_DIM, GLA_KEY_DIM), F32),
            pltpu.VMEM((c, GLA_KEY_WIDTH), F32),
            pltpu.VMEM((c, GLA_KEY_WIDTH), F32),
            pltpu.VMEM((c, D_MODEL), F32),
            pltpu.VMEM((c, GLA_WIDTH), F32),
        ],
        compiler_params=pltpu.CompilerParams(
            dimension_semantics=("arbitrary", "arbitrary"), vmem_limit_bytes=VMEM_LIMIT),
        name="gla_out",
    )(slab, slab, slab, slab, w2_pad, b2, low, sb_o, slab, slab, sb_norm_w, gla_norm_w, w_out,
      x2, post_w)


def _layer(x, pre_norm_w, w_in, w_alpha2, b_alpha2, sb_norm_w, gla_norm_w, w_out, post_norm_w):
    b, s, d = x.shape
    m = b * s

    t = SB_TILE
    pos = jnp.arange(t)
    tri = -(pos[:, None] >= pos[None, :]).astype(BF16)
    bias = jnp.where(pos[None, :] < pos[:, None], 0.0, SB_MASKED_LOGIT).astype(F32)
    slab3, sb_o = _inproj_sb(x, pre_norm_w.reshape(1, d), w_in.astype(BF16), tri, bias)

    tok = jnp.arange(GLA_CHUNK * GLA_CUMSUM_CHUNKS)
    same_chunk = (tok[:, None] // GLA_CHUNK) == (tok[None, :] // GLA_CHUNK)
    low = jnp.where(same_chunk & (tok[:, None] >= tok[None, :]), 1.0 / GATE_TEMP, 0.0).astype(BF16)
    w2_pad = jnp.pad(w_alpha2, ((0, LANES - GATE_RANK), (0, 0))).astype(BF16)
    out = _gla_out(slab3.reshape(m, SLAB_WIDTH), sb_o.reshape(m, SB_WIDTH), w2_pad,
                   b_alpha2.reshape(1, GLA_KEY_WIDTH), low,
                   sb_norm_w.reshape(1, SB_WIDTH) * SB_HEAD_DIM ** 0.5,
                   gla_norm_w.reshape(1, GLA_WIDTH) * GLA_VAL_DIM ** 0.5,
                   w_out.astype(BF16), x.reshape(m, d), post_norm_w.reshape(1, d), s)
    return out.reshape(b, s, d)


def kernel(x, pre_norm_w, w_in, w_alpha2, b_alpha2, sb_norm_w, gla_norm_w, w_out, post_norm_w):
    for layer in range(pre_norm_w.shape[0]):
        x = _layer(x, pre_norm_w[layer], w_in[layer], w_alpha2[layer], b_alpha2[layer],
                   sb_norm_w[layer], gla_norm_w[layer], w_out[layer], post_norm_w[layer])
    return x
```

```python
import math

import jax
import jax.numpy as jnp
from jax import lax
from jax.experimental import pallas as pl
from jax.experimental.pallas import tpu as pltpu

F32 = jnp.float32
BF16 = jnp.bfloat16

D_MODEL = 1024
SB_HEADS = 8
SB_HEAD_DIM = 128
SB_WIDTH = SB_HEADS * SB_HEAD_DIM
GLA_HEADS = 4
GLA_KEY_DIM = 128
GLA_VAL_DIM = 256
GLA_KEY_WIDTH = GLA_HEADS * GLA_KEY_DIM
GLA_WIDTH = GLA_HEADS * GLA_VAL_DIM
GATE_RANK = 16
GATE_TEMP = 16.0
NORM_EPS = 1e-6
LANES = 128
SUBLANES = 8

OFF_SB_Q = 0
OFF_SB_K = OFF_SB_Q + SB_WIDTH
OFF_SB_V = OFF_SB_K + SB_WIDTH
OFF_SB_G = OFF_SB_V + SB_WIDTH
OFF_GLA_Q = OFF_SB_G + SB_WIDTH
OFF_GLA_K = OFF_GLA_Q + GLA_KEY_WIDTH
OFF_GLA_V = OFF_GLA_K + GLA_KEY_WIDTH
OFF_GLA_G = OFF_GLA_V + GLA_WIDTH
OFF_GLA_LR = OFF_GLA_G + GLA_WIDTH
IN_PROJ_WIDTH = OFF_GLA_LR + GATE_RANK
SLAB_WIDTH = OFF_GLA_LR + LANES

LOG2E = math.log2(math.e)

INPROJ_TN = 256
SB_TILE = 256
SB_UNDERFLOW_LOG2 = -150.0
SB_MASKED_LOGIT = -1e30
SB_SOFTPLUS_CLAMP = 64.0
GLA_CHUNK = 128
GLA_CPS = 4
GLA_CUMSUM_CHUNKS = 2
GLA_SAFE_LOG_DECAY = -60.0
VMEM_LIMIT = 56 * 1024 * 1024

_NT = (((1,), (1,)), ((), ()))
_TN = (((0,), (0,)), ((), ()))


def _project_chunk(h, w_ref, c0):
    c1 = min(c0 + INPROJ_TN, IN_PROJ_WIDTH)
    r = jnp.dot(h, w_ref[:, c0:c1], preferred_element_type=F32)
    if OFF_SB_Q <= c0 < OFF_SB_K:
        r = r * (LOG2E * SB_HEAD_DIM ** -0.5)
    elif OFF_GLA_Q <= c0 < OFF_GLA_K:
        r = r * (GLA_KEY_DIM ** -0.5)
    elif OFF_SB_G <= c0 < OFF_GLA_Q or OFF_GLA_G <= c0 < OFF_GLA_LR:
        r = r * (1.0 / (1.0 + jnp.exp2(r * (-LOG2E))))
    return c1, r.astype(BF16)


def _sb_logits(q, k, bias):
    z = lax.dot_general(q, k, _NT, preferred_element_type=F32)
    if bias is not None:
        z = z + bias
    sp = jnp.maximum(z, jnp.log2(1.0 + jnp.exp2(jnp.minimum(z, SB_SOFTPLUS_CLAMP))))
    return z, sp.astype(BF16)


def _sb_weights(z, sp, tri):
    c = jnp.dot(sp, tri, preferred_element_type=F32)
    return c[:, 0:1], jnp.exp2(z + c).astype(BF16)


def _sb_tile(q, k, v, tri, bias):
    z, sp = _sb_logits(q, k, bias)
    ls, p = _sb_weights(z, sp, tri)
    return ls, jnp.dot(p, v, preferred_element_type=F32)


def _inproj_sb_kernel(x_ref, nw_ref, w_ref, tri_ref, bias_ref, sbw_ref, slab_hbm, o_ref,
                      stage_ref, acc_ref, ls_ref, kbuf, vbuf, sem, wsem):
    t = SB_TILE
    half = t // 2
    bi = pl.program_id(0)
    i = pl.program_id(1)
    nq = pl.num_programs(1)
    step = bi * nq + i
    last_step = pl.num_programs(0) * nq - 1
    slot = lax.rem(step, 2)
    prev_slot = 1 - slot

    def slab_write(step_idx):
        rows = pl.ds(pl.multiple_of(lax.rem(step_idx, nq) * t, t), t)
        s = lax.rem(step_idx, 2)
        return pltpu.make_async_copy(stage_ref.at[s], slab_hbm.at[lax.div(step_idx, nq), rows, :],
                                     wsem.at[s])

    @pl.when(step == 0)
    def _():
        stage_ref[1] = jnp.zeros(stage_ref.shape[1:], BF16)

    @pl.when(step >= 2)
    def _():
        slab_write(step - 2).wait()

    x = x_ref[0]
    ms = jnp.mean(x * x, axis=-1, keepdims=True)
    h = (x * lax.rsqrt(ms + NORM_EPS) * nw_ref[...]).astype(BF16)

    def project(c0):
        c1, r = _project_chunk(h, w_ref, c0)
        stage_ref[slot, :, c0:c1] = r

    chunk_starts = list(range(0, IN_PROJ_WIDTH, INPROJ_TN))
    for c0 in [c for c in chunk_starts if c < OFF_SB_G]:
        project(c0)
    later_chunks = iter([c for c in chunk_starts if c >= OFF_SB_G])
    stage_ref[slot, :, IN_PROJ_WIDTH:] = jnp.zeros((t, SLAB_WIDTH - IN_PROJ_WIDTH), BF16)

    prev_off = jnp.where(i > 0, 0.0, SB_MASKED_LOGIT).astype(F32)
    tri = tri_ref[...]
    tri_half = tri_ref[0:half, 0:half]

    def cols(h_idx, off):
        return slice(off + h_idx * SB_HEAD_DIM, off + (h_idx + 1) * SB_HEAD_DIM)

    def store_head(hd, rows, acc):
        ss = jnp.sum(acc * acc, axis=-1, keepdims=True)
        scale = sbw_ref[:, cols(hd, 0)] * stage_ref[slot, rows, cols(hd, OFF_SB_G)].astype(F32)
        o_ref[0, rows, cols(hd, 0)] = (acc * lax.rsqrt(ss + SB_HEAD_DIM * NORM_EPS) * scale).astype(BF16)

    logits, weights = {}, {}
    for stage in range(SB_HEADS + 2):
        if stage < SB_HEADS:
            q = stage_ref[slot, :, cols(stage, OFF_SB_Q)]
            kc = cols(stage, OFF_SB_K)
            logits[stage] = (
                _sb_logits(q[0:half], stage_ref[slot, 0:half, kc], bias_ref[0:half, 0:half]),
                _sb_logits(q[half:t], stage_ref[slot, :, kc], bias_ref[half:t, :]),
                _sb_logits(q, stage_ref[prev_slot, :, kc], None))
        if 0 <= stage - 1 < SB_HEADS:
            top, bot, prev = logits.pop(stage - 1)
            weights[stage - 1] = (_sb_weights(*top, tri_half), _sb_weights(*bot, tri),
                                  _sb_weights(*prev, tri))
        if 0 <= stage - 2 < SB_HEADS:
            hd = stage - 2
            vc = cols(hd, OFF_SB_V)
            (ls_t, p_t), (ls_b, p_b), (ls_p, p_p) = weights.pop(hd)
            pv_t = jnp.dot(p_t, stage_ref[slot, 0:half, vc], preferred_element_type=F32)
            pv_b = jnp.dot(p_b, stage_ref[slot, :, vc], preferred_element_type=F32)
            pv_p = jnp.dot(p_p, stage_ref[prev_slot, :, vc], preferred_element_type=F32)
            acc_t = pv_t + jnp.exp2(ls_t + prev_off) * pv_p[0:half]
            acc_b = pv_b + jnp.exp2(ls_b + prev_off) * pv_p[half:t]
            acc_ref[hd, 0:half] = acc_t
            acc_ref[hd, half:t] = acc_b
            store_head(hd, slice(0, half), acc_t)
            store_head(hd, slice(half, t), acc_b)
            ls_ref[hd, 0:half] = ls_t + ls_p[0:half]
            ls_ref[hd, half:t] = ls_b + ls_p[half:t]
        c0 = next(later_chunks, None)
        if c0 is not None:
            project(c0)
    for c0 in later_chunks:
        project(c0)

    more = jnp.logical_and(i >= 2, jnp.max(ls_ref[...]) > SB_UNDERFLOW_LOG2)

    @pl.when(more)
    def _():
        for hd in range(SB_HEADS):

            def tile_copies(j, hd=hd):
                rows = pl.ds(pl.multiple_of(j * t, t), t)
                return (pltpu.make_async_copy(slab_hbm.at[bi, rows, cols(hd, OFF_SB_K)], kbuf, sem.at[0]),
                        pltpu.make_async_copy(slab_hbm.at[bi, rows, cols(hd, OFF_SB_V)], vbuf, sem.at[1]))

            def cond(carry):
                j, ls, _ = carry
                return jnp.logical_and(j >= 0, jnp.max(ls) > SB_UNDERFLOW_LOG2)

            def body(carry, hd=hd, tile_copies=tile_copies):
                j, ls, acc = carry
                kc, vc = tile_copies(j)
                kc.start()
                vc.start()
                kc.wait()
                vc.wait()
                ls_j, pv = _sb_tile(stage_ref[slot, :, cols(hd, OFF_SB_Q)], kbuf[...], vbuf[...], tri, None)
                return j - 1, ls + ls_j, acc + jnp.exp2(ls) * pv

            _, _, acc = lax.while_loop(cond, body, (i - 2, ls_ref[hd], acc_ref[hd]))
            store_head(hd, slice(0, t), acc)

    slab_write(step).start()

    @pl.when(step == last_step)
    def _():
        @pl.when(step >= 1)
        def _():
            slab_write(step - 1).wait()
        slab_write(step).wait()


def _inproj_sb(x, pre_w, w_in, tri, bias, sb_norm_w):
    b, s, d = x.shape
    t = SB_TILE
    return pl.pallas_call(
        _inproj_sb_kernel,
        grid=(b, s // t),
        in_specs=[
            pl.BlockSpec((1, t, d), lambda bi, i: (bi, i, 0)),
            pl.BlockSpec((1, d), lambda bi, i: (0, 0)),
            pl.BlockSpec((d, IN_PROJ_WIDTH), lambda bi, i: (0, 0), pipeline_mode=pl.Buffered(1)),
            pl.BlockSpec((t, t), lambda bi, i: (0, 0)),
            pl.BlockSpec((t, t), lambda bi, i: (0, 0)),
            pl.BlockSpec((1, SB_WIDTH), lambda bi, i: (0, 0)),
        ],
        out_specs=[
            pl.BlockSpec(memory_space=pl.ANY),
            pl.BlockSpec((1, t, SB_WIDTH), lambda bi, i: (bi, i, 0)),
        ],
        out_shape=[
            jax.ShapeDtypeStruct((b, s, SLAB_WIDTH), BF16),
            jax.ShapeDtypeStruct((b, s, SB_WIDTH), BF16),
        ],
        scratch_shapes=[
            pltpu.VMEM((2, t, SLAB_WIDTH), BF16),
            pltpu.VMEM((SB_HEADS, t, SB_HEAD_DIM), F32),
            pltpu.VMEM((SB_HEADS, t, 1), F32),
            pltpu.VMEM((t, SB_HEAD_DIM), BF16),
            pltpu.VMEM((t, SB_HEAD_DIM), BF16),
            pltpu.SemaphoreType.DMA((2,)),
            pltpu.SemaphoreType.DMA((2,)),
        ],
        compiler_params=pltpu.CompilerParams(
            dimension_semantics=("arbitrary", "arbitrary"), vmem_limit_bytes=VMEM_LIMIT),
        name="inproj_sb",
    )(x, pre_w, w_in, tri, bias, sb_norm_w)


def _gla_gate_logits(lr, w2_ref, b2_ref):
    x = jnp.dot(lr, w2_ref[...], preferred_element_type=F32) + b2_ref[...]
    return x, jnp.min(x) > GLA_SAFE_LOG_DECAY * GATE_TEMP / GLA_CHUNK + math.log(2.0)


def _gla_decay(x, q, k, low_ref):
    c = GLA_CHUNK
    kw = GLA_KEY_WIDTH
    chunks = range(x.shape[0] // c)
    g = jnp.minimum(x, 0.0) - jnp.log(1.0 + jnp.exp2(jnp.abs(x) * (-LOG2E)))
    g_hi = g.astype(BF16)
    g_lo = (g - g_hi.astype(F32)).astype(BF16)
    g_split = jnp.concatenate([g_hi, g_lo], axis=1)
    span = GLA_CUMSUM_CHUNKS * c
    bc2 = jnp.concatenate(
        [jnp.dot(low_ref[...], g_split[r0:r0 + span, :], preferred_element_type=F32)
         for r0 in range(0, g.shape[0], span)], axis=0)
    bcum = bc2[:, :kw] + bc2[:, kw:]
    q_e = jnp.exp(bcum)
    qf = q.astype(F32)
    kf = k.astype(F32)
    b_last = [bcum[(ci + 1) * c - 1:(ci + 1) * c, :] for ci in chunks]
    return dict(
        bcum=bcum, q_e=q_e, qf=qf, kf=kf,
        q_t=(qf * q_e).astype(BF16),
        decay=[q_e[(ci + 1) * c - 1:(ci + 1) * c, :] for ci in chunks],
        k_h=[(kf[ci * c:(ci + 1) * c, :] * jnp.exp(b_last[ci] - bcum[ci * c:(ci + 1) * c, :])
              ).astype(BF16) for ci in chunks])


def _gla_finish(ctx, exact, v_of, state_ref, b_ref, kf_ref, store_o):
    c = GLA_CHUNK
    heads = range(GLA_HEADS)
    chunks = range(len(ctx["decay"]))
    ks = [slice(h * GLA_KEY_DIM, (h + 1) * GLA_KEY_DIM) for h in heads]
    rs = [slice(ci * c, (ci + 1) * c) for ci in chunks]
    q_t, k_h, decay = ctx["q_t"], ctx["k_h"], ctx["decay"]
    row = lax.broadcasted_iota(jnp.int32, (c, c), 0)
    col = lax.broadcasted_iota(jnp.int32, (c, c), 1)
    causal = col <= row

    if not exact:
        k_t = (ctx["kf"] * (1.0 / ctx["q_e"])).astype(BF16)
        sc = [[lax.dot_general(q_t[rs[ci], ks[h]], k_t[rs[ci], ks[h]], _NT,
                               preferred_element_type=F32) for h in heads] for ci in chunks]
        sc = [[jnp.where(causal, s, 0.0).astype(BF16) for s in per_chunk] for per_chunk in sc]
    else:
        b_ref[...] = ctx["bcum"]
        kf_ref[...] = ctx["kf"]

        def exact_scores(ci, h):
            qh = ctx["qf"][rs[ci], ks[h]]
            bh = ctx["bcum"][rs[ci], ks[h]]

            def key_rows(grp, sc):
                base = pl.multiple_of(ci * c + grp * SUBLANES, SUBLANES)
                k_rows = kf_ref[pl.ds(base, SUBLANES), ks[h]]
                b_rows = b_ref[pl.ds(base, SUBLANES), ks[h]]
                for r in range(SUBLANES):
                    w = jnp.exp(jnp.minimum(bh - b_rows[r:r + 1, :], 0.0))
                    score_col = jnp.sum(qh * k_rows[r:r + 1, :] * w, axis=-1, keepdims=True)
                    sc = jnp.where(col == grp * SUBLANES + r, score_col, sc)
                return sc

            s = lax.fori_loop(0, c // SUBLANES, key_rows, jnp.zeros((c, c), F32))
            return jnp.where(causal, s, 0.0).astype(BF16)

        sc = [[exact_scores(ci, h) for h in heads] for ci in chunks]

    st = [state_ref[h] for h in heads]
    o = []
    for ci in chunks:
        o.append([jnp.dot(sc[ci][h], v_of(ci, h), preferred_element_type=F32)
                  + lax.dot_general(q_t[rs[ci], ks[h]], st[h].astype(BF16), _NT,
                                    preferred_element_type=F32) for h in heads])
        st = [st[h] * decay[ci][:, ks[h]] + lax.dot_general(
            v_of(ci, h), k_h[ci][:, ks[h]], _TN, preferred_element_type=F32) for h in heads]
    for h in heads:
        state_ref[h] = st[h]
    for ci in chunks:
        for h in heads:
            store_o(ci, h, o[ci][h])


def _headnorm_gate(o_ref, g_ref, w_ref, n_heads):
    width = o_ref.shape[1] // n_heads
    parts = []
    for h in range(n_heads):
        cs = slice(h * width, (h + 1) * width)
        o = o_ref[:, cs].astype(F32)
        ss = jnp.sum(o * o, axis=-1, keepdims=True)
        scale = w_ref[:, cs] * g_ref[:, cs].astype(F32)
        parts.append((o * lax.rsqrt(ss + width * NORM_EPS) * scale).astype(BF16))
    return jnp.concatenate(parts, axis=1)


def _gla_out_kernel(q_ref, k_ref, v_ref, lr_ref, w2_ref, b2_ref, low_ref,
                    sby_ref, glag_ref, glaw_ref, wout_ref, x_ref, nw_ref,
                    o_ref, state_ref, b_ref, kf_ref, ysb_ref, glao_ref):
    c = GLA_CHUNK

    @pl.when(pl.program_id(1) == 0)
    def _():
        state_ref[...] = jnp.zeros_like(state_ref)

    x_gate, factorise = _gla_gate_logits(lr_ref[...], w2_ref, b2_ref)

    def gla_v(ci, hd):
        return v_ref[ci * c:(ci + 1) * c, hd * GLA_VAL_DIM:(hd + 1) * GLA_VAL_DIM]

    def store_gla_o(ci, hd, val):
        glao_ref[ci * c:(ci + 1) * c, hd * GLA_VAL_DIM:(hd + 1) * GLA_VAL_DIM] = val

    for exact in (False, True):
        @pl.when(factorise != exact)
        def _(exact=exact):
            gla = _gla_decay(x_gate, q_ref[...], k_ref[...], low_ref)
            ysb_ref[...] = jnp.dot(sby_ref[...], wout_ref[0:SB_WIDTH, :],
                                   preferred_element_type=F32)
            _gla_finish(gla, exact, gla_v, state_ref, b_ref, kf_ref, store_gla_o)
            gla_y = _headnorm_gate(glao_ref, glag_ref, glaw_ref, GLA_HEADS)
            half = gla_y.shape[0] // 2
            for r0 in (0, half):
                rr = slice(r0, r0 + half)
                y = ysb_ref[rr, :] + jnp.dot(gla_y[rr, :], wout_ref[SB_WIDTH:, :],
                                             preferred_element_type=F32)
                ms = jnp.mean(y * y, axis=-1, keepdims=True)
                o_ref[rr, :] = x_ref[rr, :] + y * lax.rsqrt(ms + NORM_EPS) * nw_ref[...]


def _gla_out(slab, sb_y, w2_pad, b2, low, gla_norm_w, w_out, x2, post_w, seq):
    m = x2.shape[0]
    c = GLA_CHUNK * GLA_CPS
    steps = seq // c

    def rows(col_block):
        return lambda bi, i: (bi * steps + i, col_block)

    const = lambda bi, i: (0, 0)
    return pl.pallas_call(
        _gla_out_kernel,
        grid=(m // seq, steps),
        in_specs=[
            pl.BlockSpec((c, GLA_KEY_WIDTH), rows(OFF_GLA_Q // GLA_KEY_WIDTH)),
            pl.BlockSpec((c, GLA_KEY_WIDTH), rows(OFF_GLA_K // GLA_KEY_WIDTH)),
            pl.BlockSpec((c, GLA_WIDTH), rows(OFF_GLA_V // GLA_WIDTH)),
            pl.BlockSpec((c, LANES), rows(OFF_GLA_LR // LANES)),
            pl.BlockSpec((LANES, GLA_KEY_WIDTH), const),
            pl.BlockSpec((1, GLA_KEY_WIDTH), const),
            pl.BlockSpec(low.shape, const),
            pl.BlockSpec((c, SB_WIDTH), rows(0)),
            pl.BlockSpec((c, GLA_WIDTH), rows(OFF_GLA_G // GLA_WIDTH)),
            pl.BlockSpec((1, GLA_WIDTH), const),
            pl.BlockSpec((SB_WIDTH + GLA_WIDTH, D_MODEL), const),
            pl.BlockSpec((c, D_MODEL), rows(0)),
            pl.BlockSpec((1, D_MODEL), const),
        ],
        out_specs=pl.BlockSpec((c, D_MODEL), rows(0)),
        out_shape=jax.ShapeDtypeStruct((m, D_MODEL), F32),
        scratch_shapes=[
            pltpu.VMEM((GLA_HEADS, GLA_VAL_DIM, GLA_KEY_DIM), F32),
            pltpu.VMEM((c, GLA_KEY_WIDTH), F32),
            pltpu.VMEM((c, GLA_KEY_WIDTH), F32),
            pltpu.VMEM((c, D_MODEL), F32),
            pltpu.VMEM((c, GLA_WIDTH), F32),
        ],
        compiler_params=pltpu.CompilerParams(
            dimension_semantics=("arbitrary", "arbitrary"), vmem_limit_bytes=VMEM_LIMIT),
        name="gla_out",
    )(slab, slab, slab, slab, w2_pad, b2, low, sb_y, slab, gla_norm_w, w_out, x2, post_w)


def _layer(x, pre_norm_w, w_in, w_alpha2, b_alpha2, sb_norm_w, gla_norm_w, w_out, post_norm_w):
    b, s, d = x.shape
    m = b * s

    t = SB_TILE
    pos = jnp.arange(t)
    tri = -(pos[:, None] >= pos[None, :]).astype(BF16)
    bias = jnp.where(pos[None, :] < pos[:, None], 0.0, SB_MASKED_LOGIT).astype(F32)
    slab3, sb_y = _inproj_sb(x, pre_norm_w.reshape(1, d), w_in.astype(BF16), tri, bias,
                             sb_norm_w.reshape(1, SB_WIDTH) * SB_HEAD_DIM ** 0.5)

    tok = jnp.arange(GLA_CHUNK * GLA_CUMSUM_CHUNKS)
    same_chunk = (tok[:, None] // GLA_CHUNK) == (tok[None, :] // GLA_CHUNK)
    low = jnp.where(same_chunk & (tok[:, None] >= tok[None, :]), 1.0 / GATE_TEMP, 0.0).astype(BF16)
    w2_pad = jnp.pad(w_alpha2, ((0, LANES - GATE_RANK), (0, 0))).astype(BF16)
    out = _gla_out(slab3.reshape(m, SLAB_WIDTH), sb_y.reshape(m, SB_WIDTH), w2_pad,
                   b_alpha2.reshape(1, GLA_KEY_WIDTH), low,
                   gla_norm_w.reshape(1, GLA_WIDTH) * GLA_VAL_DIM ** 0.5,
                   w_out.astype(BF16), x.reshape(m, d), post_norm_w.reshape(1, d), s)
    return out.reshape(b, s, d)


def kernel(x, pre_norm_w, w_in, w_alpha2, b_alpha2, sb_norm_w, gla_norm_w, w_out, post_norm_w):
    for layer in range(pre_norm_w.shape[0]):
        x = _layer(x, pre_norm_w[layer], w_in[layer], w_alpha2[layer], b_alpha2[layer],
                   sb_norm_w[layer], gla_norm_w[layer], w_out[layer], post_norm_w[layer])
    return x
```

```python
import math

import jax
import jax.numpy as jnp
from jax import lax
from jax.experimental import pallas as pl
from jax.experimental.pallas import tpu as pltpu

F32 = jnp.float32
BF16 = jnp.bfloat16

D_MODEL = 1024
SB_HEADS = 8
SB_HEAD_DIM = 128
SB_WIDTH = SB_HEADS * SB_HEAD_DIM
GLA_HEADS = 4
GLA_KEY_DIM = 128
GLA_VAL_DIM = 256
GLA_KEY_WIDTH = GLA_HEADS * GLA_KEY_DIM
GLA_WIDTH = GLA_HEADS * GLA_VAL_DIM
GATE_RANK = 16
GATE_TEMP = 16.0
NORM_EPS = 1e-6
LANES = 128
SUBLANES = 8

OFF_SB_Q = 0
OFF_SB_K = OFF_SB_Q + SB_WIDTH
OFF_SB_V = OFF_SB_K + SB_WIDTH
OFF_SB_G = OFF_SB_V + SB_WIDTH
OFF_GLA_Q = OFF_SB_G + SB_WIDTH
OFF_GLA_K = OFF_GLA_Q + GLA_KEY_WIDTH
OFF_GLA_V = OFF_GLA_K + GLA_KEY_WIDTH
OFF_GLA_G = OFF_GLA_V + GLA_WIDTH
OFF_GLA_LR = OFF_GLA_G + GLA_WIDTH
IN_PROJ_WIDTH = OFF_GLA_LR + GATE_RANK
SLAB_WIDTH = OFF_GLA_LR + LANES

LOG2E = math.log2(math.e)

INPROJ_TN = 256
SB_TILE = 256
SB_UNDERFLOW_LOG2 = -150.0
SB_MASKED_LOGIT = -1e30
SB_SOFTPLUS_CLAMP = 64.0
GLA_CHUNK = 128
GLA_CPS = 4
GLA_CUMSUM_CHUNKS = 2
GLA_SAFE_LOG_DECAY = -60.0
V7X_VMEM_BYTES = 64 * 1024 * 1024
VMEM_LIMIT = V7X_VMEM_BYTES * 7 // 8

_NT = (((1,), (1,)), ((), ()))
_TN = (((0,), (0,)), ((), ()))


def _project_chunk(h, w_ref, c0):
    c1 = min(c0 + INPROJ_TN, IN_PROJ_WIDTH)
    r = jnp.dot(h, w_ref[:, c0:c1], preferred_element_type=F32)
    if OFF_SB_Q <= c0 < OFF_SB_K:
        r = r * (LOG2E * SB_HEAD_DIM ** -0.5)
    elif OFF_GLA_Q <= c0 < OFF_GLA_K:
        r = r * (GLA_KEY_DIM ** -0.5)
    elif OFF_SB_G <= c0 < OFF_GLA_Q or OFF_GLA_G <= c0 < OFF_GLA_LR:
        r = r * (1.0 / (1.0 + jnp.exp2(r * (-LOG2E))))
    return c1, r.astype(BF16)


def _sb_logits(q, k, bias):
    z = lax.dot_general(q, k, _NT, preferred_element_type=F32)
    if bias is not None:
        z = z + bias
    sp = jnp.maximum(z, jnp.log2(1.0 + jnp.exp2(jnp.minimum(z, SB_SOFTPLUS_CLAMP))))
    return z, sp.astype(BF16)


def _sb_weights(z, sp, tri):
    c = jnp.dot(sp, tri, preferred_element_type=F32)
    return c[:, 0:1], jnp.exp2(z + c).astype(BF16)


def _sb_tile(q, k, v, tri, bias):
    z, sp = _sb_logits(q, k, bias)
    ls, p = _sb_weights(z, sp, tri)
    return ls, jnp.dot(p, v, preferred_element_type=F32)


def _inproj_sb_kernel(x_ref, nw_ref, w_ref, tri_ref, bias_ref, sbw_ref, slab_hbm, o_ref,
                      stage_ref, acc_ref, ls_ref, kbuf, vbuf, sem, wsem):
    t = SB_TILE
    half = t // 2
    bi = pl.program_id(0)
    i = pl.program_id(1)
    nq = pl.num_programs(1)
    step = bi * nq + i
    last_step = pl.num_programs(0) * nq - 1
    slot = lax.rem(step, 2)
    prev_slot = 1 - slot

    def slab_write(step_idx):
        rows = pl.ds(pl.multiple_of(lax.rem(step_idx, nq) * t, t), t)
        s = lax.rem(step_idx, 2)
        return pltpu.make_async_copy(stage_ref.at[s], slab_hbm.at[lax.div(step_idx, nq), rows, :],
                                     wsem.at[s])

    @pl.when(step == 0)
    def _():
        stage_ref[1] = jnp.zeros(stage_ref.shape[1:], BF16)

    @pl.when(step >= 2)
    def _():
        slab_write(step - 2).wait()

    x = x_ref[0]
    ms = jnp.mean(x * x, axis=-1, keepdims=True)
    h = (x * lax.rsqrt(ms + NORM_EPS) * nw_ref[...]).astype(BF16)

    def project(c0):
        c1, r = _project_chunk(h, w_ref, c0)
        stage_ref[slot, :, c0:c1] = r

    chunk_starts = list(range(0, IN_PROJ_WIDTH, INPROJ_TN))
    for c0 in [c for c in chunk_starts if c < OFF_SB_G]:
        project(c0)
    later_chunks = iter([c for c in chunk_starts if c >= OFF_SB_G])
    stage_ref[slot, :, IN_PROJ_WIDTH:] = jnp.zeros((t, SLAB_WIDTH - IN_PROJ_WIDTH), BF16)

    prev_off = jnp.where(i > 0, 0.0, SB_MASKED_LOGIT).astype(F32)
    tri = tri_ref[...]
    tri_half = tri_ref[0:half, 0:half]

    def cols(h_idx, off):
        return slice(off + h_idx * SB_HEAD_DIM, off + (h_idx + 1) * SB_HEAD_DIM)

    def store_head(hd, rows, acc):
        ss = jnp.sum(acc * acc, axis=-1, keepdims=True)
        scale = sbw_ref[:, cols(hd, 0)] * stage_ref[slot, rows, cols(hd, OFF_SB_G)].astype(F32)
        o_ref[0, rows, cols(hd, 0)] = (acc * lax.rsqrt(ss + SB_HEAD_DIM * NORM_EPS) * scale).astype(BF16)

    logits, weights = {}, {}
    for stage in range(SB_HEADS + 2):
        if stage < SB_HEADS:
            q = stage_ref[slot, :, cols(stage, OFF_SB_Q)]
            kc = cols(stage, OFF_SB_K)
            logits[stage] = (
                _sb_logits(q[0:half], stage_ref[slot, 0:half, kc], bias_ref[0:half, 0:half]),
                _sb_logits(q[half:t], stage_ref[slot, :, kc], bias_ref[half:t, :]),
                _sb_logits(q, stage_ref[prev_slot, :, kc], None))
        if 0 <= stage - 1 < SB_HEADS:
            top, bot, prev = logits.pop(stage - 1)
            weights[stage - 1] = (_sb_weights(*top, tri_half), _sb_weights(*bot, tri),
                                  _sb_weights(*prev, tri))
        if 0 <= stage - 2 < SB_HEADS:
            hd = stage - 2
            vc = cols(hd, OFF_SB_V)
            (ls_t, p_t), (ls_b, p_b), (ls_p, p_p) = weights.pop(hd)
            pv_t = jnp.dot(p_t, stage_ref[slot, 0:half, vc], preferred_element_type=F32)
            pv_b = jnp.dot(p_b, stage_ref[slot, :, vc], preferred_element_type=F32)
            pv_p = jnp.dot(p_p, stage_ref[prev_slot, :, vc], preferred_element_type=F32)
            acc_t = pv_t + jnp.exp2(ls_t + prev_off) * pv_p[0:half]
            acc_b = pv_b + jnp.exp2(ls_b + prev_off) * pv_p[half:t]
            acc_ref[hd, 0:half] = acc_t
            acc_ref[hd, half:t] = acc_b
            store_head(hd, slice(0, half), acc_t)
            store_head(hd, slice(half, t), acc_b)
            ls_ref[hd, 0:half] = ls_t + ls_p[0:half]
            ls_ref[hd, half:t] = ls_b + ls_p[half:t]
        c0 = next(later_chunks, None)
        if c0 is not None:
            project(c0)
    for c0 in later_chunks:
        project(c0)

    more = jnp.logical_and(i >= 2, jnp.max(ls_ref[...]) > SB_UNDERFLOW_LOG2)

    @pl.when(more)
    def _():
        for hd in range(SB_HEADS):

            def tile_copies(j, hd=hd):
                rows = pl.ds(pl.multiple_of(j * t, t), t)
                return (pltpu.make_async_copy(slab_hbm.at[bi, rows, cols(hd, OFF_SB_K)], kbuf, sem.at[0]),
                        pltpu.make_async_copy(slab_hbm.at[bi, rows, cols(hd, OFF_SB_V)], vbuf, sem.at[1]))

            def cond(carry):
                j, ls, _ = carry
                return jnp.logical_and(j >= 0, jnp.max(ls) > SB_UNDERFLOW_LOG2)

            def body(carry, hd=hd, tile_copies=tile_copies):
                j, ls, acc = carry
                kc, vc = tile_copies(j)
                kc.start()
                vc.start()
                kc.wait()
                vc.wait()
                ls_j, pv = _sb_tile(stage_ref[slot, :, cols(hd, OFF_SB_Q)], kbuf[...], vbuf[...], tri, None)
                return j - 1, ls + ls_j, acc + jnp.exp2(ls) * pv

            _, _, acc = lax.while_loop(cond, body, (i - 2, ls_ref[hd], acc_ref[hd]))
            store_head(hd, slice(0, t), acc)

    slab_write(step).start()

    @pl.when(step == last_step)
    def _():
        @pl.when(step >= 1)
        def _():
            slab_write(step - 1).wait()
        slab_write(step).wait()


def _inproj_sb(x, pre_w, w_in, tri, bias, sb_norm_w):
    b, s, d = x.shape
    t = SB_TILE
    return pl.pallas_call(
        _inproj_sb_kernel,
        grid=(b, s // t),
        in_specs=[
            pl.BlockSpec((1, t, d), lambda bi, i: (bi, i, 0)),
            pl.BlockSpec((1, d), lambda bi, i: (0, 0)),
            pl.BlockSpec((d, IN_PROJ_WIDTH), lambda bi, i: (0, 0), pipeline_mode=pl.Buffered(1)),
            pl.BlockSpec((t, t), lambda bi, i: (0, 0)),
            pl.BlockSpec((t, t), lambda bi, i: (0, 0)),
            pl.BlockSpec((1, SB_WIDTH), lambda bi, i: (0, 0)),
        ],
        out_specs=[
            pl.BlockSpec(memory_space=pl.ANY),
            pl.BlockSpec((1, t, SB_WIDTH), lambda bi, i: (bi, i, 0)),
        ],
        out_shape=[
            jax.ShapeDtypeStruct((b, s, SLAB_WIDTH), BF16),
            jax.ShapeDtypeStruct((b, s, SB_WIDTH), BF16),
        ],
        scratch_shapes=[
            pltpu.VMEM((2, t, SLAB_WIDTH), BF16),
            pltpu.VMEM((SB_HEADS, t, SB_HEAD_DIM), F32),
            pltpu.VMEM((SB_HEADS, t, 1), F32),
            pltpu.VMEM((t, SB_HEAD_DIM), BF16),
            pltpu.VMEM((t, SB_HEAD_DIM), BF16),
            pltpu.SemaphoreType.DMA((2,)),
            pltpu.SemaphoreType.DMA((2,)),
        ],
        compiler_params=pltpu.CompilerParams(
            dimension_semantics=("arbitrary", "arbitrary"), vmem_limit_bytes=VMEM_LIMIT),
        name="inproj_sb",
    )(x, pre_w, w_in, tri, bias, sb_norm_w)


def _gla_gate_logits(lr, w2_ref, b2_ref):
    x = jnp.dot(lr, w2_ref[...], preferred_element_type=F32) + b2_ref[...]
    return x, jnp.min(x) > GLA_SAFE_LOG_DECAY * GATE_TEMP / GLA_CHUNK + math.log(2.0)


def _gla_decay(x, q, k, low_ref, exact):
    c = GLA_CHUNK
    kw = GLA_KEY_WIDTH
    chunks = range(x.shape[0] // c)
    g = jnp.minimum(x, 0.0) - jnp.log(1.0 + jnp.exp2(jnp.abs(x) * (-LOG2E)))
    g_hi = g.astype(BF16)
    g_lo = (g - g_hi.astype(F32)).astype(BF16)
    g_split = jnp.concatenate([g_hi, g_lo], axis=1)
    span = GLA_CUMSUM_CHUNKS * c
    bc2 = jnp.concatenate(
        [jnp.dot(low_ref[...], g_split[r0:r0 + span, :], preferred_element_type=F32)
         for r0 in range(0, g.shape[0], span)], axis=0)
    bcum = bc2[:, :kw] + bc2[:, kw:]
    q_e = jnp.exp(bcum)
    qf = q.astype(F32)
    kf = k.astype(F32)
    rs = [slice(ci * c, (ci + 1) * c) for ci in chunks]
    decay = [q_e[(ci + 1) * c - 1:(ci + 1) * c, :] for ci in chunks]
    if exact:
        k_t = None
        k_h = [(kf[rs[ci], :] * jnp.exp(bcum[(ci + 1) * c - 1:(ci + 1) * c, :] - bcum[rs[ci], :])
                ).astype(BF16) for ci in chunks]
    else:
        kk = kf * (1.0 / q_e)
        k_t = kk.astype(BF16)
        k_h = [(kk[rs[ci], :] * decay[ci]).astype(BF16) for ci in chunks]
    return dict(bcum=bcum, qf=qf, kf=kf, q_t=(qf * q_e).astype(BF16), decay=decay, k_t=k_t, k_h=k_h)


def _gla_finish(ctx, exact, v_of, state_ref, b_ref, kf_ref, store_o, fillers=()):
    c = GLA_CHUNK
    heads = range(GLA_HEADS)
    chunks = range(len(ctx["decay"]))
    ks = [slice(h * GLA_KEY_DIM, (h + 1) * GLA_KEY_DIM) for h in heads]
    rs = [slice(ci * c, (ci + 1) * c) for ci in chunks]
    q_t, k_h, decay = ctx["q_t"], ctx["k_h"], ctx["decay"]
    fillers = iter(fillers)

    def fill():
        emit = next(fillers, None)
        if emit is not None:
            emit()

    row = lax.broadcasted_iota(jnp.int32, (c, c), 0)
    col = lax.broadcasted_iota(jnp.int32, (c, c), 1)
    causal = col <= row

    if not exact:
        k_t = ctx["k_t"]
        sc = [[lax.dot_general(q_t[rs[ci], ks[h]], k_t[rs[ci], ks[h]], _NT,
                               preferred_element_type=F32) for h in heads] for ci in chunks]
        fill()
        sc = [[jnp.where(causal, s, 0.0).astype(BF16) for s in per_chunk] for per_chunk in sc]
    else:
        b_ref[...] = ctx["bcum"]
        kf_ref[...] = ctx["kf"]

        def exact_scores(ci, h):
            qh = ctx["qf"][rs[ci], ks[h]]
            bh = ctx["bcum"][rs[ci], ks[h]]

            def key_rows(grp, sc):
                base = pl.multiple_of(ci * c + grp * SUBLANES, SUBLANES)
                k_rows = kf_ref[pl.ds(base, SUBLANES), ks[h]]
                b_rows = b_ref[pl.ds(base, SUBLANES), ks[h]]
                for r in range(SUBLANES):
                    w = jnp.exp(jnp.minimum(bh - b_rows[r:r + 1, :], 0.0))
                    score_col = jnp.sum(qh * k_rows[r:r + 1, :] * w, axis=-1, keepdims=True)
                    sc = jnp.where(col == grp * SUBLANES + r, score_col, sc)
                return sc

            s = lax.fori_loop(0, c // SUBLANES, key_rows, jnp.zeros((c, c), F32))
            return jnp.where(causal, s, 0.0).astype(BF16)

        sc = [[exact_scores(ci, h) for h in heads] for ci in chunks]
        fill()

    st = [state_ref[h] for h in heads]
    o = []
    for ci in chunks:
        o.append([jnp.dot(sc[ci][h], v_of(ci, h), preferred_element_type=F32)
                  + lax.dot_general(q_t[rs[ci], ks[h]], st[h].astype(BF16), _NT,
                                    preferred_element_type=F32) for h in heads])
        st = [st[h] * decay[ci][:, ks[h]] + lax.dot_general(
            v_of(ci, h), k_h[ci][:, ks[h]], _TN, preferred_element_type=F32) for h in heads]
        fill()
    for emit in fillers:
        emit()
    for h in heads:
        state_ref[h] = st[h]
    for ci in chunks:
        for h in heads:
            store_o(ci, h, o[ci][h])


def _headnorm_gate(o_ref, g_ref, w_ref, n_heads):
    width = o_ref.shape[1] // n_heads
    parts = []
    for h in range(n_heads):
        cs = slice(h * width, (h + 1) * width)
        o = o_ref[:, cs].astype(F32)
        ss = jnp.sum(o * o, axis=-1, keepdims=True)
        scale = w_ref[:, cs] * g_ref[:, cs].astype(F32)
        parts.append((o * lax.rsqrt(ss + width * NORM_EPS) * scale).astype(BF16))
    return jnp.concatenate(parts, axis=1)


def _gla_out_kernel(q_ref, k_ref, v_ref, lr_ref, w2_ref, b2_ref, low_ref,
                    sby_ref, glag_ref, glaw_ref, wout_ref, x_ref, nw_ref,
                    o_ref, state_ref, b_ref, kf_ref, ysb_ref, glao_ref):
    c = GLA_CHUNK

    @pl.when(pl.program_id(1) == 0)
    def _():
        state_ref[...] = jnp.zeros_like(state_ref)

    x_gate, factorise = _gla_gate_logits(lr_ref[...], w2_ref, b2_ref)

    def gla_v(ci, hd):
        return v_ref[ci * c:(ci + 1) * c, hd * GLA_VAL_DIM:(hd + 1) * GLA_VAL_DIM]

    def store_gla_o(ci, hd, val):
        glao_ref[ci * c:(ci + 1) * c, hd * GLA_VAL_DIM:(hd + 1) * GLA_VAL_DIM] = val

    for exact in (False, True):
        @pl.when(factorise != exact)
        def _(exact=exact):
            gla = _gla_decay(x_gate, q_ref[...], k_ref[...], low_ref, exact)
            def sb_outproj(r0):
                def emit():
                    ysb_ref[r0:r0 + c, :] = jnp.dot(sby_ref[r0:r0 + c, :], wout_ref[0:SB_WIDTH, :],
                                                    preferred_element_type=F32)
                return emit

            _gla_finish(gla, exact, gla_v, state_ref, b_ref, kf_ref, store_gla_o,
                        fillers=[sb_outproj(r0) for r0 in range(0, GLA_CPS * c, c)])
            gla_y = _headnorm_gate(glao_ref, glag_ref, glaw_ref, GLA_HEADS)
            half = gla_y.shape[0] // 2
            for r0 in (0, half):
                rr = slice(r0, r0 + half)
                y = ysb_ref[rr, :] + jnp.dot(gla_y[rr, :], wout_ref[SB_WIDTH:, :],
                                             preferred_element_type=F32)
                ms = jnp.mean(y * y, axis=-1, keepdims=True)
                o_ref[rr, :] = x_ref[rr, :] + y * lax.rsqrt(ms + NORM_EPS) * nw_ref[...]


def _gla_out(slab, sb_y, w2_pad, b2, low, gla_norm_w, w_out, x2, post_w, seq):
    m = x2.shape[0]
    c = GLA_CHUNK * GLA_CPS
    steps = seq // c

    def rows(col_block):
        return lambda bi, i: (bi * steps + i, col_block)

    const = lambda bi, i: (0, 0)
    return pl.pallas_call(
        _gla_out_kernel,
        grid=(m // seq, steps),
        in_specs=[
            pl.BlockSpec((c, GLA_KEY_WIDTH), rows(OFF_GLA_Q // GLA_KEY_WIDTH)),
            pl.BlockSpec((c, GLA_KEY_WIDTH), rows(OFF_GLA_K // GLA_KEY_WIDTH)),
            pl.BlockSpec((c, GLA_WIDTH), rows(OFF_GLA_V // GLA_WIDTH)),
            pl.BlockSpec((c, LANES), rows(OFF_GLA_LR // LANES)),
            pl.BlockSpec((LANES, GLA_KEY_WIDTH), const),
            pl.BlockSpec((1, GLA_KEY_WIDTH), const),
            pl.BlockSpec(low.shape, const),
            pl.BlockSpec((c, SB_WIDTH), rows(0)),
            pl.BlockSpec((c, GLA_WIDTH), rows(OFF_GLA_G // GLA_WIDTH)),
            pl.BlockSpec((1, GLA_WIDTH), const),
            pl.BlockSpec((SB_WIDTH + GLA_WIDTH, D_MODEL), const),
            pl.BlockSpec((c, D_MODEL), rows(0)),
            pl.BlockSpec((1, D_MODEL), const),
        ],
        out_specs=pl.BlockSpec((c, D_MODEL), rows(0)),
        out_shape=jax.ShapeDtypeStruct((m, D_MODEL), F32),
        scratch_shapes=[
            pltpu.VMEM((GLA_HEADS, GLA_VAL_DIM, GLA_KEY_DIM), F32),
            pltpu.VMEM((c, GLA_KEY_WIDTH), F32),
            pltpu.VMEM((c, GLA_KEY_WIDTH), F32),
            pltpu.VMEM((c, D_MODEL), F32),
            pltpu.VMEM((c, GLA_WIDTH), F32),
        ],
        compiler_params=pltpu.CompilerParams(
            dimension_semantics=("arbitrary", "arbitrary"), vmem_limit_bytes=VMEM_LIMIT),
        name="gla_out",
    )(slab, slab, slab, slab, w2_pad, b2, low, sb_y, slab, gla_norm_w, w_out, x2, post_w)


def _layer(x, pre_norm_w, w_in, w_alpha2, b_alpha2, sb_norm_w, gla_norm_w, w_out, post_norm_w):
    b, s, d = x.shape
    m = b * s

    t = SB_TILE
    pos = jnp.arange(t)
    tri = -(pos[:, None] >= pos[None, :]).astype(BF16)
    bias = jnp.where(pos[None, :] < pos[:, None], 0.0, SB_MASKED_LOGIT).astype(F32)
    slab3, sb_y = _inproj_sb(x, pre_norm_w.reshape(1, d), w_in.astype(BF16), tri, bias,
                             sb_norm_w.reshape(1, SB_WIDTH) * SB_HEAD_DIM ** 0.5)

    tok = jnp.arange(GLA_CHUNK * GLA_CUMSUM_CHUNKS)
    same_chunk = (tok[:, None] // GLA_CHUNK) == (tok[None, :] // GLA_CHUNK)
    low = jnp.where(same_chunk & (tok[:, None] >= tok[None, :]), 1.0 / GATE_TEMP, 0.0).astype(BF16)
    w2_pad = jnp.pad(w_alpha2, ((0, LANES - GATE_RANK), (0, 0))).astype(BF16)
    out = _gla_out(slab3.reshape(m, SLAB_WIDTH), sb_y.reshape(m, SB_WIDTH), w2_pad,
                   b_alpha2.reshape(1, GLA_KEY_WIDTH), low,
                   gla_norm_w.reshape(1, GLA_WIDTH) * GLA_VAL_DIM ** 0.5,
                   w_out.astype(BF16), x.reshape(m, d), post_norm_w.reshape(1, d), s)
    return out.reshape(b, s, d)


def kernel(x, pre_norm_w, w_in, w_alpha2, b_alpha2, sb_norm_w, gla_norm_w, w_out, post_norm_w):
    for layer in range(pre_norm_w.shape[0]):
        x = _layer(x, pre_norm_w[layer], w_in[layer], w_alpha2[layer], b_alpha2[layer],
                   sb_norm_w[layer], gla_norm_w[layer], w_out[layer], post_norm_w[layer])
    return x
```

```python
import math

import jax
import jax.numpy as jnp
from jax import lax
from jax.experimental import pallas as pl
from jax.experimental.pallas import tpu as pltpu

F32 = jnp.float32
BF16 = jnp.bfloat16

D_MODEL = 1024
SB_HEADS = 8
SB_HEAD_DIM = 128
SB_WIDTH = SB_HEADS * SB_HEAD_DIM
GLA_HEADS = 4
GLA_KEY_DIM = 128
GLA_VAL_DIM = 256
GLA_KEY_WIDTH = GLA_HEADS * GLA_KEY_DIM
GLA_WIDTH = GLA_HEADS * GLA_VAL_DIM
GATE_RANK = 16
GATE_TEMP = 16.0
NORM_EPS = 1e-6
LANES = 128
SUBLANES = 8

OFF_SB_Q = 0
OFF_SB_K = OFF_SB_Q + SB_WIDTH
OFF_SB_V = OFF_SB_K + SB_WIDTH
OFF_SB_G = OFF_SB_V + SB_WIDTH
OFF_GLA_Q = OFF_SB_G + SB_WIDTH
OFF_GLA_K = OFF_GLA_Q + GLA_KEY_WIDTH
OFF_GLA_V = OFF_GLA_K + GLA_KEY_WIDTH
OFF_GLA_G = OFF_GLA_V + GLA_WIDTH
OFF_GLA_LR = OFF_GLA_G + GLA_WIDTH
IN_PROJ_WIDTH = OFF_GLA_LR + GATE_RANK
OFF_GLA_GHI = OFF_GLA_LR
OFF_GLA_GLO = OFF_GLA_GHI + GLA_KEY_WIDTH
SLAB_WIDTH = OFF_GLA_GLO + GLA_KEY_WIDTH

LOG2E = math.log2(math.e)

INPROJ_TN = 256
SB_TILE = 256
SB_UNDERFLOW_LOG2 = -150.0
SB_MASKED_LOGIT = -1e30
SB_SOFTPLUS_CLAMP = 64.0
GLA_CHUNK = 128
GLA_CPS = 4
GLA_CUMSUM_CHUNKS = 2
GLA_SAFE_LOG_DECAY = -60.0
V7X_VMEM_BYTES = 64 * 1024 * 1024
VMEM_LIMIT = V7X_VMEM_BYTES * 7 // 8

_NT = (((1,), (1,)), ((), ()))
_TN = (((0,), (0,)), ((), ()))


def _project_chunk(h, w_ref, c0):
    c1 = min(c0 + INPROJ_TN, IN_PROJ_WIDTH)
    r = jnp.dot(h, w_ref[:, c0:c1], preferred_element_type=F32)
    if OFF_SB_Q <= c0 < OFF_SB_K:
        r = r * (LOG2E * SB_HEAD_DIM ** -0.5)
    elif OFF_GLA_Q <= c0 < OFF_GLA_K:
        r = r * (GLA_KEY_DIM ** -0.5)
    elif OFF_SB_G <= c0 < OFF_GLA_Q or OFF_GLA_G <= c0 < OFF_GLA_LR:
        r = r * (1.0 / (1.0 + jnp.exp2(r * (-LOG2E))))
    return c1, r.astype(BF16)


def _sb_logits(q, k, bias):
    z = lax.dot_general(q, k, _NT, preferred_element_type=F32)
    if bias is not None:
        z = z + bias
    sp = jnp.maximum(z, jnp.log2(1.0 + jnp.exp2(jnp.minimum(z, SB_SOFTPLUS_CLAMP))))
    return z, sp.astype(BF16)


def _sb_weights(z, sp, tri):
    c = jnp.dot(sp, tri, preferred_element_type=F32)
    return c[:, 0:1], jnp.exp2(z + c).astype(BF16)


def _sb_tile(q, k, v, tri, bias):
    z, sp = _sb_logits(q, k, bias)
    ls, p = _sb_weights(z, sp, tri)
    return ls, jnp.dot(p, v, preferred_element_type=F32)


def _inproj_sb_kernel(x_ref, nw_ref, w_ref, tri_ref, bias_ref, sbw_ref, w2_ref, b2_ref,
                      slab_hbm, o_ref, gate_min_ref,
                      stage_ref, lr_ref, acc_ref, ls_ref, kbuf, vbuf, sem, wsem):
    t = SB_TILE
    half = t // 2
    bi = pl.program_id(0)
    i = pl.program_id(1)
    nq = pl.num_programs(1)
    step = bi * nq + i
    last_step = pl.num_programs(0) * nq - 1
    slot = lax.rem(step, 2)
    prev_slot = 1 - slot

    def slab_write(step_idx):
        rows = pl.ds(pl.multiple_of(lax.rem(step_idx, nq) * t, t), t)
        s = lax.rem(step_idx, 2)
        return pltpu.make_async_copy(stage_ref.at[s], slab_hbm.at[lax.div(step_idx, nq), rows, :],
                                     wsem.at[s])

    @pl.when(step == 0)
    def _():
        stage_ref[1] = jnp.zeros(stage_ref.shape[1:], BF16)
        lr_ref[...] = jnp.zeros(lr_ref.shape, BF16)

    @pl.when(step >= 2)
    def _():
        slab_write(step - 2).wait()

    x = x_ref[0]
    ms = jnp.mean(x * x, axis=-1, keepdims=True)
    h = (x * lax.rsqrt(ms + NORM_EPS) * nw_ref[...]).astype(BF16)

    def project(c0):
        c1, r = _project_chunk(h, w_ref, c0)
        stage_ref[slot, :, c0:c1] = r

    def project_gate_rank():
        _, r = _project_chunk(h, w_ref, OFF_GLA_LR)
        lr_ref[:, 0:GATE_RANK] = r

    def finish_gate():
        x_gate = _gla_gate_logits(lr_ref[...], w2_ref, b2_ref)
        gate_min_ref[step] = jnp.min(x_gate)
        g = jnp.minimum(x_gate, 0.0) - jnp.log(1.0 + jnp.exp2(jnp.abs(x_gate) * (-LOG2E)))
        g_hi = g.astype(BF16)
        stage_ref[slot, :, OFF_GLA_GHI:OFF_GLA_GLO] = g_hi
        stage_ref[slot, :, OFF_GLA_GLO:SLAB_WIDTH] = (g - g_hi.astype(F32)).astype(BF16)

    chunk_starts = list(range(0, IN_PROJ_WIDTH, INPROJ_TN))
    for c0 in [c for c in chunk_starts if c < OFF_SB_G]:
        project(c0)
    later_chunks = iter([c for c in chunk_starts if OFF_SB_G <= c < OFF_GLA_LR])

    prev_off = jnp.where(i > 0, 0.0, SB_MASKED_LOGIT).astype(F32)
    tri = tri_ref[...]
    tri_half = tri_ref[0:half, 0:half]

    def cols(h_idx, off):
        return slice(off + h_idx * SB_HEAD_DIM, off + (h_idx + 1) * SB_HEAD_DIM)

    def store_head(hd, rows, acc):
        ss = jnp.sum(acc * acc, axis=-1, keepdims=True)
        scale = sbw_ref[:, cols(hd, 0)] * stage_ref[slot, rows, cols(hd, OFF_SB_G)].astype(F32)
        o_ref[0, rows, cols(hd, 0)] = (acc * lax.rsqrt(ss + SB_HEAD_DIM * NORM_EPS) * scale).astype(BF16)

    logits, weights = {}, {}
    for stage in range(SB_HEADS + 2):
        if stage < SB_HEADS:
            q = stage_ref[slot, :, cols(stage, OFF_SB_Q)]
            kc = cols(stage, OFF_SB_K)
            logits[stage] = (
                _sb_logits(q[0:half], stage_ref[slot, 0:half, kc], bias_ref[0:half, 0:half]),
                _sb_logits(q[half:t], stage_ref[slot, :, kc], bias_ref[half:t, :]),
                _sb_logits(q, stage_ref[prev_slot, :, kc], None))
        if 0 <= stage - 1 < SB_HEADS:
            top, bot, prev = logits.pop(stage - 1)
            weights[stage - 1] = (_sb_weights(*top, tri_half), _sb_weights(*bot, tri),
                                  _sb_weights(*prev, tri))
        if 0 <= stage - 2 < SB_HEADS:
            hd = stage - 2
            vc = cols(hd, OFF_SB_V)
            (ls_t, p_t), (ls_b, p_b), (ls_p, p_p) = weights.pop(hd)
            pv_t = jnp.dot(p_t, stage_ref[slot, 0:half, vc], preferred_element_type=F32)
            pv_b = jnp.dot(p_b, stage_ref[slot, :, vc], preferred_element_type=F32)
            pv_p = jnp.dot(p_p, stage_ref[prev_slot, :, vc], preferred_element_type=F32)
            acc_t = pv_t + jnp.exp2(ls_t + prev_off) * pv_p[0:half]
            acc_b = pv_b + jnp.exp2(ls_b + prev_off) * pv_p[half:t]
            acc_ref[hd, 0:half] = acc_t
            acc_ref[hd, half:t] = acc_b
            store_head(hd, slice(0, half), acc_t)
            store_head(hd, slice(half, t), acc_b)
            ls_ref[hd, 0:half] = ls_t + ls_p[0:half]
            ls_ref[hd, half:t] = ls_b + ls_p[half:t]
        c0 = next(later_chunks, None)
        if c0 is not None:
            project(c0)
    project_gate_rank()
    trailing = list(later_chunks)
    for c0 in trailing[:len(trailing) // 2]:
        project(c0)
    finish_gate()
    for c0 in trailing[len(trailing) // 2:]:
        project(c0)

    more = jnp.logical_and(i >= 2, jnp.max(ls_ref[...]) > SB_UNDERFLOW_LOG2)

    @pl.when(more)
    def _():
        for hd in range(SB_HEADS):

            def tile_copies(j, hd=hd):
                rows = pl.ds(pl.multiple_of(j * t, t), t)
                return (pltpu.make_async_copy(slab_hbm.at[bi, rows, cols(hd, OFF_SB_K)], kbuf, sem.at[0]),
                        pltpu.make_async_copy(slab_hbm.at[bi, rows, cols(hd, OFF_SB_V)], vbuf, sem.at[1]))

            def cond(carry):
                j, ls, _ = carry
                return jnp.logical_and(j >= 0, jnp.max(ls) > SB_UNDERFLOW_LOG2)

            def body(carry, hd=hd, tile_copies=tile_copies):
                j, ls, acc = carry
                kc, vc = tile_copies(j)
                kc.start()
                vc.start()
                kc.wait()
                vc.wait()
                ls_j, pv = _sb_tile(stage_ref[slot, :, cols(hd, OFF_SB_Q)], kbuf[...], vbuf[...], tri, None)
                return j - 1, ls + ls_j, acc + jnp.exp2(ls) * pv

            _, _, acc = lax.while_loop(cond, body, (i - 2, ls_ref[hd], acc_ref[hd]))
            store_head(hd, slice(0, t), acc)

    slab_write(step).start()

    @pl.when(step == last_step)
    def _():
        @pl.when(step >= 1)
        def _():
            slab_write(step - 1).wait()
        slab_write(step).wait()


def _inproj_sb(x, pre_w, w_in, tri, bias, sb_norm_w, w2_pad, b2):
    b, s, d = x.shape
    t = SB_TILE
    return pl.pallas_call(
        _inproj_sb_kernel,
        grid=(b, s // t),
        in_specs=[
            pl.BlockSpec((1, t, d), lambda bi, i: (bi, i, 0)),
            pl.BlockSpec((1, d), lambda bi, i: (0, 0)),
            pl.BlockSpec((d, IN_PROJ_WIDTH), lambda bi, i: (0, 0), pipeline_mode=pl.Buffered(1)),
            pl.BlockSpec((t, t), lambda bi, i: (0, 0)),
            pl.BlockSpec((t, t), lambda bi, i: (0, 0)),
            pl.BlockSpec((1, SB_WIDTH), lambda bi, i: (0, 0)),
            pl.BlockSpec((LANES, GLA_KEY_WIDTH), lambda bi, i: (0, 0)),
            pl.BlockSpec((1, GLA_KEY_WIDTH), lambda bi, i: (0, 0)),
        ],
        out_specs=[
            pl.BlockSpec(memory_space=pl.ANY),
            pl.BlockSpec((1, t, SB_WIDTH), lambda bi, i: (bi, i, 0)),
            pl.BlockSpec(memory_space=pltpu.SMEM),
        ],
        out_shape=[
            jax.ShapeDtypeStruct((b, s, SLAB_WIDTH), BF16),
            jax.ShapeDtypeStruct((b, s, SB_WIDTH), BF16),
            jax.ShapeDtypeStruct((b * (s // t),), F32),
        ],
        scratch_shapes=[
            pltpu.VMEM((2, t, SLAB_WIDTH), BF16),
            pltpu.VMEM((t, LANES), BF16),
            pltpu.VMEM((SB_HEADS, t, SB_HEAD_DIM), F32),
            pltpu.VMEM((SB_HEADS, t, 1), F32),
            pltpu.VMEM((t, SB_HEAD_DIM), BF16),
            pltpu.VMEM((t, SB_HEAD_DIM), BF16),
            pltpu.SemaphoreType.DMA((2,)),
            pltpu.SemaphoreType.DMA((2,)),
        ],
        compiler_params=pltpu.CompilerParams(
            dimension_semantics=("arbitrary", "arbitrary"), vmem_limit_bytes=VMEM_LIMIT),
        name="inproj_sb",
    )(x, pre_w, w_in, tri, bias, sb_norm_w, w2_pad, b2)


GLA_SAFE_GATE_LOGIT = GLA_SAFE_LOG_DECAY * GATE_TEMP / GLA_CHUNK + math.log(2.0)


def _gla_gate_logits(lr, w2_ref, b2_ref):
    return jnp.dot(lr, w2_ref[...], preferred_element_type=F32) + b2_ref[...]


def _gla_decay(g_hi, g_lo, q, k, low_ref, exact):
    c = GLA_CHUNK
    kw = GLA_KEY_WIDTH
    chunks = range(g_hi.shape[0] // c)
    g_split = jnp.concatenate([g_hi, g_lo], axis=1)
    span = GLA_CUMSUM_CHUNKS * c
    bc2 = jnp.concatenate(
        [jnp.dot(low_ref[...], g_split[r0:r0 + span, :], preferred_element_type=F32)
         for r0 in range(0, g_hi.shape[0], span)], axis=0)
    bcum = bc2[:, :kw] + bc2[:, kw:]
    q_e = jnp.exp(bcum)
    qf = q.astype(F32)
    kf = k.astype(F32)
    rs = [slice(ci * c, (ci + 1) * c) for ci in chunks]
    decay = [q_e[(ci + 1) * c - 1:(ci + 1) * c, :] for ci in chunks]
    if exact:
        k_t = None
        k_h = [(kf[rs[ci], :] * jnp.exp(bcum[(ci + 1) * c - 1:(ci + 1) * c, :] - bcum[rs[ci], :])
                ).astype(BF16) for ci in chunks]
    else:
        kk = kf * (1.0 / q_e)
        k_t = kk.astype(BF16)
        k_h = [(kk[rs[ci], :] * decay[ci]).astype(BF16) for ci in chunks]
    return dict(bcum=bcum, qf=qf, kf=kf, q_t=(qf * q_e).astype(BF16), decay=decay, k_t=k_t, k_h=k_h)


def _gla_finish(ctx, exact, v_of, state_ref, b_ref, kf_ref, store_o, fillers=()):
    c = GLA_CHUNK
    heads = range(GLA_HEADS)
    chunks = range(len(ctx["decay"]))
    ks = [slice(h * GLA_KEY_DIM, (h + 1) * GLA_KEY_DIM) for h in heads]
    rs = [slice(ci * c, (ci + 1) * c) for ci in chunks]
    q_t, k_h, decay = ctx["q_t"], ctx["k_h"], ctx["decay"]
    fillers = iter(fillers)

    def fill():
        emit = next(fillers, None)
        if emit is not None:
            emit()

    row = lax.broadcasted_iota(jnp.int32, (c, c), 0)
    col = lax.broadcasted_iota(jnp.int32, (c, c), 1)
    causal = col <= row

    if not exact:
        k_t = ctx["k_t"]
        sc = [[lax.dot_general(q_t[rs[ci], ks[h]], k_t[rs[ci], ks[h]], _NT,
                               preferred_element_type=F32) for h in heads] for ci in chunks]
        fill()
        sc = [[jnp.where(causal, s, 0.0).astype(BF16) for s in per_chunk] for per_chunk in sc]
    else:
        b_ref[...] = ctx["bcum"]
        kf_ref[...] = ctx["kf"]

        def exact_scores(ci, h):
            qh = ctx["qf"][rs[ci], ks[h]]
            bh = ctx["bcum"][rs[ci], ks[h]]

            def key_rows(grp, sc):
                base = pl.multiple_of(ci * c + grp * SUBLANES, SUBLANES)
                k_rows = kf_ref[pl.ds(base, SUBLANES), ks[h]]
                b_rows = b_ref[pl.ds(base, SUBLANES), ks[h]]
                for r in range(SUBLANES):
                    w = jnp.exp(jnp.minimum(bh - b_rows[r:r + 1, :], 0.0))
                    score_col = jnp.sum(qh * k_rows[r:r + 1, :] * w, axis=-1, keepdims=True)
                    sc = jnp.where(col == grp * SUBLANES + r, score_col, sc)
                return sc

            s = lax.fori_loop(0, c // SUBLANES, key_rows, jnp.zeros((c, c), F32))
            return jnp.where(causal, s, 0.0).astype(BF16)

        sc = [[exact_scores(ci, h) for h in heads] for ci in chunks]
        fill()

    st = [state_ref[h] for h in heads]
    o = []
    for ci in chunks:
        o.append([jnp.dot(sc[ci][h], v_of(ci, h), preferred_element_type=F32)
                  + lax.dot_general(q_t[rs[ci], ks[h]], st[h].astype(BF16), _NT,
                                    preferred_element_type=F32) for h in heads])
        st = [st[h] * decay[ci][:, ks[h]] + lax.dot_general(
            v_of(ci, h), k_h[ci][:, ks[h]], _TN, preferred_element_type=F32) for h in heads]
        fill()
    for emit in fillers:
        emit()
    for h in heads:
        state_ref[h] = st[h]
    for ci in chunks:
        for h in heads:
            store_o(ci, h, o[ci][h])


def _headnorm_gate(o_ref, g_ref, w_ref, n_heads):
    width = o_ref.shape[1] // n_heads
    parts = []
    for h in range(n_heads):
        cs = slice(h * width, (h + 1) * width)
        o = o_ref[:, cs].astype(F32)
        ss = jnp.sum(o * o, axis=-1, keepdims=True)
        scale = w_ref[:, cs] * g_ref[:, cs].astype(F32)
        parts.append((o * lax.rsqrt(ss + width * NORM_EPS) * scale).astype(BF16))
    return jnp.concatenate(parts, axis=1)


def _gla_out_kernel(gate_min_ref, q_ref, k_ref, v_ref, ghi_ref, glo_ref, low_ref,
                    sby_ref, glag_ref, glaw_ref, wout_ref, x_ref, nw_ref,
                    o_ref, state_ref, b_ref, kf_ref, ysb_ref, glao_ref):
    c = GLA_CHUNK

    @pl.when(pl.program_id(1) == 0)
    def _():
        state_ref[...] = jnp.zeros_like(state_ref)

    tiles = (GLA_CPS * c) // SB_TILE
    first = (pl.program_id(0) * pl.num_programs(1) + pl.program_id(1)) * tiles
    gate_min = gate_min_ref[first]
    for tile in range(1, tiles):
        gate_min = jnp.minimum(gate_min, gate_min_ref[first + tile])
    factorise = gate_min > GLA_SAFE_GATE_LOGIT

    def gla_v(ci, hd):
        return v_ref[ci * c:(ci + 1) * c, hd * GLA_VAL_DIM:(hd + 1) * GLA_VAL_DIM]

    def store_gla_o(ci, hd, val):
        glao_ref[ci * c:(ci + 1) * c, hd * GLA_VAL_DIM:(hd + 1) * GLA_VAL_DIM] = val

    for exact in (False, True):
        @pl.when(factorise != exact)
        def _(exact=exact):
            gla = _gla_decay(ghi_ref[...], glo_ref[...], q_ref[...], k_ref[...], low_ref, exact)
            def sb_outproj(r0):
                def emit():
                    ysb_ref[r0:r0 + c, :] = jnp.dot(sby_ref[r0:r0 + c, :], wout_ref[0:SB_WIDTH, :],
                                                    preferred_element_type=F32)
                return emit

            _gla_finish(gla, exact, gla_v, state_ref, b_ref, kf_ref, store_gla_o,
                        fillers=[sb_outproj(r0) for r0 in range(0, GLA_CPS * c, c)])
            gla_y = _headnorm_gate(glao_ref, glag_ref, glaw_ref, GLA_HEADS)
            half = gla_y.shape[0] // 2
            for r0 in (0, half):
                rr = slice(r0, r0 + half)
                y = ysb_ref[rr, :] + jnp.dot(gla_y[rr, :], wout_ref[SB_WIDTH:, :],
                                             preferred_element_type=F32)
                ms = jnp.mean(y * y, axis=-1, keepdims=True)
                o_ref[rr, :] = x_ref[rr, :] + y * lax.rsqrt(ms + NORM_EPS) * nw_ref[...]


def _gla_out(gate_min, slab, sb_y, low, gla_norm_w, w_out, x2, post_w, seq):
    m = x2.shape[0]
    c = GLA_CHUNK * GLA_CPS
    steps = seq // c

    def rows(col_block):
        return lambda bi, i, gate_min_ref: (bi * steps + i, col_block)

    const = lambda bi, i, gate_min_ref: (0, 0)
    grid_spec = pltpu.PrefetchScalarGridSpec(
        num_scalar_prefetch=1,
        grid=(m // seq, steps),
        in_specs=[
            pl.BlockSpec((c, GLA_KEY_WIDTH), rows(OFF_GLA_Q // GLA_KEY_WIDTH)),
            pl.BlockSpec((c, GLA_KEY_WIDTH), rows(OFF_GLA_K // GLA_KEY_WIDTH)),
            pl.BlockSpec((c, GLA_WIDTH), rows(OFF_GLA_V // GLA_WIDTH)),
            pl.BlockSpec((c, GLA_KEY_WIDTH), rows(OFF_GLA_GHI // GLA_KEY_WIDTH)),
            pl.BlockSpec((c, GLA_KEY_WIDTH), rows(OFF_GLA_GLO // GLA_KEY_WIDTH)),
            pl.BlockSpec(low.shape, const),
            pl.BlockSpec((c, SB_WIDTH), rows(0)),
            pl.BlockSpec((c, GLA_WIDTH), rows(OFF_GLA_G // GLA_WIDTH)),
            pl.BlockSpec((1, GLA_WIDTH), const),
            pl.BlockSpec((SB_WIDTH + GLA_WIDTH, D_MODEL), const),
            pl.BlockSpec((c, D_MODEL), rows(0)),
            pl.BlockSpec((1, D_MODEL), const),
        ],
        out_specs=pl.BlockSpec((c, D_MODEL), rows(0)),
        scratch_shapes=[
            pltpu.VMEM((GLA_HEADS, GLA_VAL_DIM, GLA_KEY_DIM), F32),
            pltpu.VMEM((c, GLA_KEY_WIDTH), F32),
            pltpu.VMEM((c, GLA_KEY_WIDTH), F32),
            pltpu.VMEM((c, D_MODEL), F32),
            pltpu.VMEM((c, GLA_WIDTH), F32),
        ])
    return pl.pallas_call(
        _gla_out_kernel,
        grid_spec=grid_spec,
        out_shape=jax.ShapeDtypeStruct((m, D_MODEL), F32),
        compiler_params=pltpu.CompilerParams(
            dimension_semantics=("arbitrary", "arbitrary"), vmem_limit_bytes=VMEM_LIMIT),
        name="gla_out",
    )(gate_min, slab, slab, slab, slab, slab, low, sb_y, slab, gla_norm_w, w_out, x2, post_w)


def _layer(x, pre_norm_w, w_in, w_alpha2, b_alpha2, sb_norm_w, gla_norm_w, w_out, post_norm_w):
    b, s, d = x.shape
    m = b * s

    t = SB_TILE
    pos = jnp.arange(t)
    tri = -(pos[:, None] >= pos[None, :]).astype(BF16)
    bias = jnp.where(pos[None, :] < pos[:, None], 0.0, SB_MASKED_LOGIT).astype(F32)
    w2_pad = jnp.pad(w_alpha2, ((0, LANES - GATE_RANK), (0, 0))).astype(BF16)
    slab3, sb_y, gate_min = _inproj_sb(x, pre_norm_w.reshape(1, d), w_in.astype(BF16), tri, bias,
                                       sb_norm_w.reshape(1, SB_WIDTH) * SB_HEAD_DIM ** 0.5,
                                       w2_pad, b_alpha2.reshape(1, GLA_KEY_WIDTH))

    tok = jnp.arange(GLA_CHUNK * GLA_CUMSUM_CHUNKS)
    same_chunk = (tok[:, None] // GLA_CHUNK) == (tok[None, :] // GLA_CHUNK)
    low = jnp.where(same_chunk & (tok[:, None] >= tok[None, :]), 1.0 / GATE_TEMP, 0.0).astype(BF16)
    out = _gla_out(gate_min, slab3.reshape(m, SLAB_WIDTH), sb_y.reshape(m, SB_WIDTH), low,
                   gla_norm_w.reshape(1, GLA_WIDTH) * GLA_VAL_DIM ** 0.5,
                   w_out.astype(BF16), x.reshape(m, d), post_norm_w.reshape(1, d), s)
    return out.reshape(b, s, d)


def kernel(x, pre_norm_w, w_in, w_alpha2, b_alpha2, sb_norm_w, gla_norm_w, w_out, post_norm_w):
    for layer in range(pre_norm_w.shape[0]):
        x = _layer(x, pre_norm_w[layer], w_in[layer], w_alpha2[layer], b_alpha2[layer],
                   sb_norm_w[layer], gla_norm_w[layer], w_out[layer], post_norm_w[layer])
    return x
```

```python
import math

import jax
import jax.numpy as jnp
from jax import lax
from jax.experimental import pallas as pl
from jax.experimental.pallas import tpu as pltpu

F32 = jnp.float32
BF16 = jnp.bfloat16

D_MODEL = 1024
SB_HEADS = 8
SB_HEAD_DIM = 128
SB_WIDTH = SB_HEADS * SB_HEAD_DIM
GLA_HEADS = 4
GLA_KEY_DIM = 128
GLA_VAL_DIM = 256
GLA_KEY_WIDTH = GLA_HEADS * GLA_KEY_DIM
GLA_WIDTH = GLA_HEADS * GLA_VAL_DIM
GATE_RANK = 16
GATE_TEMP = 16.0
NORM_EPS = 1e-6
LANES = 128
SUBLANES = 8

OFF_SB_Q = 0
OFF_SB_K = OFF_SB_Q + SB_WIDTH
OFF_SB_V = OFF_SB_K + SB_WIDTH
OFF_SB_G = OFF_SB_V + SB_WIDTH
OFF_GLA_Q = OFF_SB_G + SB_WIDTH
OFF_GLA_K = OFF_GLA_Q + GLA_KEY_WIDTH
OFF_GLA_V = OFF_GLA_K + GLA_KEY_WIDTH
OFF_GLA_G = OFF_GLA_V + GLA_WIDTH
OFF_GLA_LR = OFF_GLA_G + GLA_WIDTH
IN_PROJ_WIDTH = OFF_GLA_LR + GATE_RANK
SLAB_WIDTH = OFF_GLA_Q
GLA_PROJ_WIDTH = OFF_GLA_LR + LANES - OFF_GLA_Q

LOG2E = math.log2(math.e)

INPROJ_TN = 256
SB_TILE = 256
SB_UNDERFLOW_LOG2 = -150.0
SB_MASKED_LOGIT = -1e30
SB_SOFTPLUS_CLAMP = 64.0
GLA_CHUNK = 128
GLA_CPS = 4
GLA_CUMSUM_CHUNKS = 2
GLA_SAFE_LOG_DECAY = -60.0
V7X_VMEM_BYTES = 64 * 1024 * 1024
VMEM_LIMIT = V7X_VMEM_BYTES * 7 // 8

_NT = (((1,), (1,)), ((), ()))
_TN = (((0,), (0,)), ((), ()))


def _project_chunk(h, w_ref, c0, w_col0=0):
    c1 = min(c0 + INPROJ_TN, IN_PROJ_WIDTH)
    r = jnp.dot(h, w_ref[:, c0 - w_col0:c1 - w_col0], preferred_element_type=F32)
    if OFF_SB_Q <= c0 < OFF_SB_K:
        r = r * (LOG2E * SB_HEAD_DIM ** -0.5)
    elif OFF_GLA_Q <= c0 < OFF_GLA_K:
        r = r * (GLA_KEY_DIM ** -0.5)
    elif OFF_SB_G <= c0 < OFF_GLA_Q or OFF_GLA_G <= c0 < OFF_GLA_LR:
        r = r * (1.0 / (1.0 + jnp.exp2(r * (-LOG2E))))
    return c1, r.astype(BF16)


def _sb_logits(q, k, bias):
    z = lax.dot_general(q, k, _NT, preferred_element_type=F32)
    if bias is not None:
        z = z + bias
    sp = jnp.maximum(z, jnp.log2(1.0 + jnp.exp2(jnp.minimum(z, SB_SOFTPLUS_CLAMP))))
    return z, sp.astype(BF16)


def _sb_weights(z, sp, tri):
    c = jnp.dot(sp, tri, preferred_element_type=F32)
    return c[:, 0:1], jnp.exp2(z + c).astype(BF16)


def _sb_tile(q, k, v, tri, bias):
    z, sp = _sb_logits(q, k, bias)
    ls, p = _sb_weights(z, sp, tri)
    return ls, jnp.dot(p, v, preferred_element_type=F32)


def _inproj_sb_kernel(x_ref, nw_ref, w_ref, tri_ref, bias_ref, sbw_ref, slab_hbm, o_ref,
                      stage_ref, acc_ref, ls_ref, kbuf, vbuf, sem, wsem):
    t = SB_TILE
    half = t // 2
    bi = pl.program_id(0)
    i = pl.program_id(1)
    nq = pl.num_programs(1)
    step = bi * nq + i
    last_step = pl.num_programs(0) * nq - 1
    slot = lax.rem(step, 2)
    prev_slot = 1 - slot

    def slab_write(step_idx):
        rows = pl.ds(pl.multiple_of(lax.rem(step_idx, nq) * t, t), t)
        s = lax.rem(step_idx, 2)
        return pltpu.make_async_copy(stage_ref.at[s], slab_hbm.at[lax.div(step_idx, nq), rows, :],
                                     wsem.at[s])

    @pl.when(step == 0)
    def _():
        stage_ref[1] = jnp.zeros(stage_ref.shape[1:], BF16)

    @pl.when(step >= 2)
    def _():
        slab_write(step - 2).wait()

    x = x_ref[0]
    ms = jnp.mean(x * x, axis=-1, keepdims=True)
    h = (x * lax.rsqrt(ms + NORM_EPS) * nw_ref[...]).astype(BF16)

    def project(c0):
        c1, r = _project_chunk(h, w_ref, c0)
        stage_ref[slot, :, c0:c1] = r

    chunk_starts = list(range(0, SLAB_WIDTH, INPROJ_TN))
    for c0 in [c for c in chunk_starts if c < OFF_SB_G]:
        project(c0)
    later_chunks = iter([c for c in chunk_starts if c >= OFF_SB_G])

    prev_off = jnp.where(i > 0, 0.0, SB_MASKED_LOGIT).astype(F32)
    tri = tri_ref[...]
    tri_half = tri_ref[0:half, 0:half]

    def cols(h_idx, off):
        return slice(off + h_idx * SB_HEAD_DIM, off + (h_idx + 1) * SB_HEAD_DIM)

    def store_head(hd, rows, acc):
        ss = jnp.sum(acc * acc, axis=-1, keepdims=True)
        scale = sbw_ref[:, cols(hd, 0)] * stage_ref[slot, rows, cols(hd, OFF_SB_G)].astype(F32)
        o_ref[0, rows, cols(hd, 0)] = (acc * lax.rsqrt(ss + SB_HEAD_DIM * NORM_EPS) * scale).astype(BF16)

    logits, weights = {}, {}
    for stage in range(SB_HEADS + 2):
        if stage < SB_HEADS:
            q = stage_ref[slot, :, cols(stage, OFF_SB_Q)]
            kc = cols(stage, OFF_SB_K)
            logits[stage] = (
                _sb_logits(q[0:half], stage_ref[slot, 0:half, kc], bias_ref[0:half, 0:half]),
                _sb_logits(q[half:t], stage_ref[slot, :, kc], bias_ref[half:t, :]),
                _sb_logits(q, stage_ref[prev_slot, :, kc], None))
        if 0 <= stage - 1 < SB_HEADS:
            top, bot, prev = logits.pop(stage - 1)
            weights[stage - 1] = (_sb_weights(*top, tri_half), _sb_weights(*bot, tri),
                                  _sb_weights(*prev, tri))
        if 0 <= stage - 2 < SB_HEADS:
            hd = stage - 2
            vc = cols(hd, OFF_SB_V)
            (ls_t, p_t), (ls_b, p_b), (ls_p, p_p) = weights.pop(hd)
            pv_t = jnp.dot(p_t, stage_ref[slot, 0:half, vc], preferred_element_type=F32)
            pv_b = jnp.dot(p_b, stage_ref[slot, :, vc], preferred_element_type=F32)
            pv_p = jnp.dot(p_p, stage_ref[prev_slot, :, vc], preferred_element_type=F32)
            acc_t = pv_t + jnp.exp2(ls_t + prev_off) * pv_p[0:half]
            acc_b = pv_b + jnp.exp2(ls_b + prev_off) * pv_p[half:t]
            acc_ref[hd, 0:half] = acc_t
            acc_ref[hd, half:t] = acc_b
            store_head(hd, slice(0, half), acc_t)
            store_head(hd, slice(half, t), acc_b)
            ls_ref[hd, 0:half] = ls_t + ls_p[0:half]
            ls_ref[hd, half:t] = ls_b + ls_p[half:t]
        c0 = next(later_chunks, None)
        if c0 is not None:
            project(c0)
    for c0 in later_chunks:
        project(c0)

    more = jnp.logical_and(i >= 2, jnp.max(ls_ref[...]) > SB_UNDERFLOW_LOG2)

    @pl.when(more)
    def _():
        for hd in range(SB_HEADS):

            def tile_copies(j, hd=hd):
                rows = pl.ds(pl.multiple_of(j * t, t), t)
                return (pltpu.make_async_copy(slab_hbm.at[bi, rows, cols(hd, OFF_SB_K)], kbuf, sem.at[0]),
                        pltpu.make_async_copy(slab_hbm.at[bi, rows, cols(hd, OFF_SB_V)], vbuf, sem.at[1]))

            def cond(carry):
                j, ls, _ = carry
                return jnp.logical_and(j >= 0, jnp.max(ls) > SB_UNDERFLOW_LOG2)

            def body(carry, hd=hd, tile_copies=tile_copies):
                j, ls, acc = carry
                kc, vc = tile_copies(j)
                kc.start()
                vc.start()
                kc.wait()
                vc.wait()
                ls_j, pv = _sb_tile(stage_ref[slot, :, cols(hd, OFF_SB_Q)], kbuf[...], vbuf[...], tri, None)
                return j - 1, ls + ls_j, acc + jnp.exp2(ls) * pv

            _, _, acc = lax.while_loop(cond, body, (i - 2, ls_ref[hd], acc_ref[hd]))
            store_head(hd, slice(0, t), acc)

    slab_write(step).start()

    @pl.when(step == last_step)
    def _():
        @pl.when(step >= 1)
        def _():
            slab_write(step - 1).wait()
        slab_write(step).wait()


def _inproj_sb(x, pre_w, w_in, tri, bias, sb_norm_w):
    b, s, d = x.shape
    t = SB_TILE
    return pl.pallas_call(
        _inproj_sb_kernel,
        grid=(b, s // t),
        in_specs=[
            pl.BlockSpec((1, t, d), lambda bi, i: (bi, i, 0)),
            pl.BlockSpec((1, d), lambda bi, i: (0, 0)),
            pl.BlockSpec((d, SLAB_WIDTH), lambda bi, i: (0, 0), pipeline_mode=pl.Buffered(1)),
            pl.BlockSpec((t, t), lambda bi, i: (0, 0)),
            pl.BlockSpec((t, t), lambda bi, i: (0, 0)),
            pl.BlockSpec((1, SB_WIDTH), lambda bi, i: (0, 0)),
        ],
        out_specs=[
            pl.BlockSpec(memory_space=pl.ANY),
            pl.BlockSpec((1, t, SB_WIDTH), lambda bi, i: (bi, i, 0)),
        ],
        out_shape=[
            jax.ShapeDtypeStruct((b, s, SLAB_WIDTH), BF16),
            jax.ShapeDtypeStruct((b, s, SB_WIDTH), BF16),
        ],
        scratch_shapes=[
            pltpu.VMEM((2, t, SLAB_WIDTH), BF16),
            pltpu.VMEM((SB_HEADS, t, SB_HEAD_DIM), F32),
            pltpu.VMEM((SB_HEADS, t, 1), F32),
            pltpu.VMEM((t, SB_HEAD_DIM), BF16),
            pltpu.VMEM((t, SB_HEAD_DIM), BF16),
            pltpu.SemaphoreType.DMA((2,)),
            pltpu.SemaphoreType.DMA((2,)),
        ],
        compiler_params=pltpu.CompilerParams(
            dimension_semantics=("arbitrary", "arbitrary"), vmem_limit_bytes=VMEM_LIMIT),
        name="inproj_sb",
    )(x, pre_w, w_in, tri, bias, sb_norm_w)


def _gla_gate_logits(lr, w2_ref, b2_ref):
    x = jnp.dot(lr, w2_ref[...], preferred_element_type=F32) + b2_ref[...]
    return x, jnp.min(x) > GLA_SAFE_LOG_DECAY * GATE_TEMP / GLA_CHUNK + math.log(2.0)


def _gla_decay(x, q, k, low_ref, exact):
    c = GLA_CHUNK
    kw = GLA_KEY_WIDTH
    chunks = range(x.shape[0] // c)
    g = jnp.minimum(x, 0.0) - jnp.log(1.0 + jnp.exp2(jnp.abs(x) * (-LOG2E)))
    g_hi = g.astype(BF16)
    g_lo = (g - g_hi.astype(F32)).astype(BF16)
    g_split = jnp.concatenate([g_hi, g_lo], axis=1)
    span = GLA_CUMSUM_CHUNKS * c
    bc2 = jnp.concatenate(
        [jnp.dot(low_ref[...], g_split[r0:r0 + span, :], preferred_element_type=F32)
         for r0 in range(0, g.shape[0], span)], axis=0)
    bcum = bc2[:, :kw] + bc2[:, kw:]
    q_e = jnp.exp(bcum)
    qf = q.astype(F32)
    kf = k.astype(F32)
    rs = [slice(ci * c, (ci + 1) * c) for ci in chunks]
    decay = [q_e[(ci + 1) * c - 1:(ci + 1) * c, :] for ci in chunks]
    if exact:
        k_t = None
        k_h = [(kf[rs[ci], :] * jnp.exp(bcum[(ci + 1) * c - 1:(ci + 1) * c, :] - bcum[rs[ci], :])
                ).astype(BF16) for ci in chunks]
    else:
        kk = kf * (1.0 / q_e)
        k_t = kk.astype(BF16)
        k_h = [(kk[rs[ci], :] * decay[ci]).astype(BF16) for ci in chunks]
    return dict(bcum=bcum, qf=qf, kf=kf, q_t=(qf * q_e).astype(BF16), decay=decay, k_t=k_t, k_h=k_h)


def _gla_finish(ctx, exact, v_of, state_ref, b_ref, kf_ref, store_o, fillers=()):
    c = GLA_CHUNK
    heads = range(GLA_HEADS)
    chunks = range(len(ctx["decay"]))
    ks = [slice(h * GLA_KEY_DIM, (h + 1) * GLA_KEY_DIM) for h in heads]
    rs = [slice(ci * c, (ci + 1) * c) for ci in chunks]
    q_t, k_h, decay = ctx["q_t"], ctx["k_h"], ctx["decay"]
    fillers = iter(fillers)

    def fill():
        emit = next(fillers, None)
        if emit is not None:
            emit()

    row = lax.broadcasted_iota(jnp.int32, (c, c), 0)
    col = lax.broadcasted_iota(jnp.int32, (c, c), 1)
    causal = col <= row

    if not exact:
        k_t = ctx["k_t"]
        sc = [[lax.dot_general(q_t[rs[ci], ks[h]], k_t[rs[ci], ks[h]], _NT,
                               preferred_element_type=F32) for h in heads] for ci in chunks]
        fill()
        sc = [[jnp.where(causal, s, 0.0).astype(BF16) for s in per_chunk] for per_chunk in sc]
    else:
        b_ref[...] = ctx["bcum"]
        kf_ref[...] = ctx["kf"]

        def exact_scores(ci, h):
            qh = ctx["qf"][rs[ci], ks[h]]
            bh = ctx["bcum"][rs[ci], ks[h]]

            def key_rows(grp, sc):
                base = pl.multiple_of(ci * c + grp * SUBLANES, SUBLANES)
                k_rows = kf_ref[pl.ds(base, SUBLANES), ks[h]]
                b_rows = b_ref[pl.ds(base, SUBLANES), ks[h]]
                for r in range(SUBLANES):
                    w = jnp.exp(jnp.minimum(bh - b_rows[r:r + 1, :], 0.0))
                    score_col = jnp.sum(qh * k_rows[r:r + 1, :] * w, axis=-1, keepdims=True)
                    sc = jnp.where(col == grp * SUBLANES + r, score_col, sc)
                return sc

            s = lax.fori_loop(0, c // SUBLANES, key_rows, jnp.zeros((c, c), F32))
            return jnp.where(causal, s, 0.0).astype(BF16)

        sc = [[exact_scores(ci, h) for h in heads] for ci in chunks]
        fill()

    st = [state_ref[h] for h in heads]
    o = []
    for ci in chunks:
        o.append([jnp.dot(sc[ci][h], v_of(ci, h), preferred_element_type=F32)
                  + lax.dot_general(q_t[rs[ci], ks[h]], st[h].astype(BF16), _NT,
                                    preferred_element_type=F32) for h in heads])
        st = [st[h] * decay[ci][:, ks[h]] + lax.dot_general(
            v_of(ci, h), k_h[ci][:, ks[h]], _TN, preferred_element_type=F32) for h in heads]
        fill()
    for emit in fillers:
        emit()
    for h in heads:
        state_ref[h] = st[h]
    for ci in chunks:
        for h in heads:
            store_o(ci, h, o[ci][h])


def _headnorm_gate(o_ref, g_ref, w_ref, n_heads):
    width = o_ref.shape[1] // n_heads
    parts = []
    for h in range(n_heads):
        cs = slice(h * width, (h + 1) * width)
        o = o_ref[:, cs].astype(F32)
        ss = jnp.sum(o * o, axis=-1, keepdims=True)
        scale = w_ref[:, cs] * g_ref[:, cs].astype(F32)
        parts.append((o * lax.rsqrt(ss + width * NORM_EPS) * scale).astype(BF16))
    return jnp.concatenate(parts, axis=1)


def _gla_out_kernel(x_ref, pnw_ref, wg_ref, w2_ref, b2_ref, low_ref,
                    sby_ref, glaw_ref, wout_ref, nw_ref,
                    o_ref, state_ref, b_ref, kf_ref, ysb_ref, glao_ref, proj_ref):
    c = GLA_CHUNK
    kw = GLA_KEY_WIDTH
    q0, k0, v0, g0, lr0 = (off - OFF_GLA_Q for off in
                           (OFF_GLA_Q, OFF_GLA_K, OFF_GLA_V, OFF_GLA_G, OFF_GLA_LR))

    @pl.when(pl.program_id(1) == 0)
    def _():
        state_ref[...] = jnp.zeros_like(state_ref)

    x = x_ref[...]
    ms = jnp.mean(x * x, axis=-1, keepdims=True)
    h = (x * lax.rsqrt(ms + NORM_EPS) * pnw_ref[...]).astype(BF16)

    def project(c0):
        c1, r = _project_chunk(h, wg_ref, c0, OFF_GLA_Q)
        proj_ref[:, c0 - OFF_GLA_Q:c1 - OFF_GLA_Q] = r

    proj_ref[:, IN_PROJ_WIDTH - OFF_GLA_Q:] = jnp.zeros(
        (x.shape[0], GLA_PROJ_WIDTH - (IN_PROJ_WIDTH - OFF_GLA_Q)), BF16)
    project(OFF_GLA_LR)
    x_gate, factorise = _gla_gate_logits(proj_ref[:, lr0:lr0 + LANES], w2_ref, b2_ref)
    for c0 in range(OFF_GLA_Q, OFF_GLA_LR, INPROJ_TN):
        project(c0)

    def gla_v(ci, hd):
        return proj_ref[ci * c:(ci + 1) * c, v0 + hd * GLA_VAL_DIM:v0 + (hd + 1) * GLA_VAL_DIM]

    def store_gla_o(ci, hd, val):
        glao_ref[ci * c:(ci + 1) * c, hd * GLA_VAL_DIM:(hd + 1) * GLA_VAL_DIM] = val

    for exact in (False, True):
        @pl.when(factorise != exact)
        def _(exact=exact):
            gla = _gla_decay(x_gate, proj_ref[:, q0:q0 + kw], proj_ref[:, k0:k0 + kw], low_ref, exact)
            def sb_outproj(r0):
                def emit():
                    ysb_ref[r0:r0 + c, :] = jnp.dot(sby_ref[r0:r0 + c, :], wout_ref[0:SB_WIDTH, :],
                                                    preferred_element_type=F32)
                return emit

            _gla_finish(gla, exact, gla_v, state_ref, b_ref, kf_ref, store_gla_o,
                        fillers=[sb_outproj(r0) for r0 in range(0, GLA_CPS * c, c)])
            gla_y = _headnorm_gate(glao_ref, proj_ref.at[:, pl.ds(g0, GLA_WIDTH)], glaw_ref, GLA_HEADS)
            half = gla_y.shape[0] // 2
            for r0 in (0, half):
                rr = slice(r0, r0 + half)
                y = ysb_ref[rr, :] + jnp.dot(gla_y[rr, :], wout_ref[SB_WIDTH:, :],
                                             preferred_element_type=F32)
                ms2 = jnp.mean(y * y, axis=-1, keepdims=True)
                o_ref[rr, :] = x_ref[rr, :] + y * lax.rsqrt(ms2 + NORM_EPS) * nw_ref[...]


def _gla_out(x2, pre_w, w_gla, sb_y, w2_pad, b2, low, gla_norm_w, w_out, post_w, seq):
    m = x2.shape[0]
    c = GLA_CHUNK * GLA_CPS
    steps = seq // c
    rows = lambda bi, i: (bi * steps + i, 0)
    const = lambda bi, i: (0, 0)
    return pl.pallas_call(
        _gla_out_kernel,
        grid=(m // seq, steps),
        in_specs=[
            pl.BlockSpec((c, D_MODEL), rows),
            pl.BlockSpec((1, D_MODEL), const),
            pl.BlockSpec(w_gla.shape, const, pipeline_mode=pl.Buffered(1)),
            pl.BlockSpec((LANES, GLA_KEY_WIDTH), const),
            pl.BlockSpec((1, GLA_KEY_WIDTH), const),
            pl.BlockSpec(low.shape, const),
            pl.BlockSpec((c, SB_WIDTH), rows),
            pl.BlockSpec((1, GLA_WIDTH), const),
            pl.BlockSpec((SB_WIDTH + GLA_WIDTH, D_MODEL), const, pipeline_mode=pl.Buffered(1)),
            pl.BlockSpec((1, D_MODEL), const),
        ],
        out_specs=pl.BlockSpec((c, D_MODEL), rows),
        out_shape=jax.ShapeDtypeStruct((m, D_MODEL), F32),
        scratch_shapes=[
            pltpu.VMEM((GLA_HEADS, GLA_VAL_DIM, GLA_KEY_DIM), F32),
            pltpu.VMEM((c, GLA_KEY_WIDTH), F32),
            pltpu.VMEM((c, GLA_KEY_WIDTH), F32),
            pltpu.VMEM((c, D_MODEL), F32),
            pltpu.VMEM((c, GLA_WIDTH), F32),
            pltpu.VMEM((c, GLA_PROJ_WIDTH), BF16),
        ],
        compiler_params=pltpu.CompilerParams(
            dimension_semantics=("arbitrary", "arbitrary"), vmem_limit_bytes=VMEM_LIMIT),
        name="gla_out",
    )(x2, pre_w, w_gla, w2_pad, b2, low, sb_y, gla_norm_w, w_out, post_w)


def _layer(x, pre_norm_w, w_in, w_alpha2, b_alpha2, sb_norm_w, gla_norm_w, w_out, post_norm_w):
    b, s, d = x.shape
    m = b * s

    t = SB_TILE
    pos = jnp.arange(t)
    tri = -(pos[:, None] >= pos[None, :]).astype(BF16)
    bias = jnp.where(pos[None, :] < pos[:, None], 0.0, SB_MASKED_LOGIT).astype(F32)
    w_bf = w_in.astype(BF16)
    pre_w = pre_norm_w.reshape(1, d)
    _, sb_y = _inproj_sb(x, pre_w, w_bf[:, :SLAB_WIDTH], tri, bias,
                         sb_norm_w.reshape(1, SB_WIDTH) * SB_HEAD_DIM ** 0.5)

    tok = jnp.arange(GLA_CHUNK * GLA_CUMSUM_CHUNKS)
    same_chunk = (tok[:, None] // GLA_CHUNK) == (tok[None, :] // GLA_CHUNK)
    low = jnp.where(same_chunk & (tok[:, None] >= tok[None, :]), 1.0 / GATE_TEMP, 0.0).astype(BF16)
    w2_pad = jnp.pad(w_alpha2, ((0, LANES - GATE_RANK), (0, 0))).astype(BF16)
    out = _gla_out(x.reshape(m, d), pre_w, w_bf[:, SLAB_WIDTH:], sb_y.reshape(m, SB_WIDTH), w2_pad,
                   b_alpha2.reshape(1, GLA_KEY_WIDTH), low,
                   gla_norm_w.reshape(1, GLA_WIDTH) * GLA_VAL_DIM ** 0.5,
                   w_out.astype(BF16), post_norm_w.reshape(1, d), s)
    return out.reshape(b, s, d)


def kernel(x, pre_norm_w, w_in, w_alpha2, b_alpha2, sb_norm_w, gla_norm_w, w_out, post_norm_w):
    for layer in range(pre_norm_w.shape[0]):
        x = _layer(x, pre_norm_w[layer], w_in[layer], w_alpha2[layer], b_alpha2[layer],
                   sb_norm_w[layer], gla_norm_w[layer], w_out[layer], post_norm_w[layer])
    return x
```

```python
import math

import jax
import jax.numpy as jnp
from jax import lax
from jax.experimental import pallas as pl
from jax.experimental.pallas import tpu as pltpu

F32 = jnp.float32
BF16 = jnp.bfloat16

D_MODEL = 1024
SB_HEADS = 8
SB_HEAD_DIM = 128
SB_WIDTH = SB_HEADS * SB_HEAD_DIM
GLA_HEADS = 4
GLA_KEY_DIM = 128
GLA_VAL_DIM = 256
GLA_KEY_WIDTH = GLA_HEADS * GLA_KEY_DIM
GLA_WIDTH = GLA_HEADS * GLA_VAL_DIM
GATE_RANK = 16
GATE_TEMP = 16.0
NORM_EPS = 1e-6
LANES = 128
SUBLANES = 8

OFF_SB_Q = 0
OFF_SB_K = OFF_SB_Q + SB_WIDTH
OFF_SB_V = OFF_SB_K + SB_WIDTH
OFF_SB_G = OFF_SB_V + SB_WIDTH
OFF_GLA_Q = OFF_SB_G + SB_WIDTH
OFF_GLA_K = OFF_GLA_Q + GLA_KEY_WIDTH
OFF_GLA_V = OFF_GLA_K + GLA_KEY_WIDTH
OFF_GLA_G = OFF_GLA_V + GLA_WIDTH
OFF_GLA_LR = OFF_GLA_G + GLA_WIDTH
IN_PROJ_WIDTH = OFF_GLA_LR + GATE_RANK
SLAB_WIDTH = OFF_GLA_LR + LANES

LOG2E = math.log2(math.e)

INPROJ_TN = 256
SB_TILE = 256
SB_UNDERFLOW_LOG2 = -150.0
SB_MASKED_LOGIT = -1e30
SB_SOFTPLUS_CLAMP = 64.0
GLA_CHUNK = 128
GLA_CPS = 4
GLA_CUMSUM_CHUNKS = 2
GLA_SAFE_LOG_DECAY = -60.0
V7X_VMEM_BYTES = 64 * 1024 * 1024
VMEM_LIMIT = V7X_VMEM_BYTES * 7 // 8

_NT = (((1,), (1,)), ((), ()))
_TN = (((0,), (0,)), ((), ()))


def _project_chunk(h, w_ref, c0):
    c1 = min(c0 + INPROJ_TN, IN_PROJ_WIDTH)
    r = jnp.dot(h, w_ref[:, c0:c1], preferred_element_type=F32)
    if OFF_SB_Q <= c0 < OFF_SB_K:
        r = r * (LOG2E * SB_HEAD_DIM ** -0.5)
    elif OFF_GLA_Q <= c0 < OFF_GLA_K:
        r = r * (GLA_KEY_DIM ** -0.5)
    elif OFF_SB_G <= c0 < OFF_GLA_Q or OFF_GLA_G <= c0 < OFF_GLA_LR:
        r = r * (1.0 / (1.0 + jnp.exp2(r * (-LOG2E))))
    return c1, r.astype(BF16)


def _sb_logits(q, k, bias):
    z = lax.dot_general(q, k, _NT, preferred_element_type=F32)
    if bias is not None:
        z = z + bias
    sp = jnp.maximum(z, jnp.log2(1.0 + jnp.exp2(jnp.minimum(z, SB_SOFTPLUS_CLAMP))))
    return z, sp.astype(BF16)


def _sb_weights(z, sp, tri):
    c = jnp.dot(sp, tri, preferred_element_type=F32)
    return c[:, 0:1], jnp.exp2(z + c).astype(BF16)


def _sb_tile(q, k, v, tri, bias):
    z, sp = _sb_logits(q, k, bias)
    ls, p = _sb_weights(z, sp, tri)
    return ls, jnp.dot(p, v, preferred_element_type=F32)


def _inproj_sb_kernel(x_ref, nw_ref, w_ref, tri_ref, bias_ref, sbw_ref, slab_hbm, o_ref,
                      stage_ref, acc_ref, ls_ref, kbuf, vbuf, sem, wsem):
    t = SB_TILE
    half = t // 2
    bi = pl.program_id(0)
    i = pl.program_id(1)
    nq = pl.num_programs(1)
    step = bi * nq + i
    last_step = pl.num_programs(0) * nq - 1
    slot = lax.rem(step, 2)
    prev_slot = 1 - slot

    def slab_write(step_idx):
        rows = pl.ds(pl.multiple_of(lax.rem(step_idx, nq) * t, t), t)
        s = lax.rem(step_idx, 2)
        return pltpu.make_async_copy(stage_ref.at[s], slab_hbm.at[lax.div(step_idx, nq), rows, :],
                                     wsem.at[s])

    @pl.when(step == 0)
    def _():
        stage_ref[1] = jnp.zeros(stage_ref.shape[1:], BF16)

    @pl.when(step >= 2)
    def _():
        slab_write(step - 2).wait()

    x = x_ref[0]
    ms = jnp.mean(x * x, axis=-1, keepdims=True)
    h = (x * lax.rsqrt(ms + NORM_EPS) * nw_ref[...]).astype(BF16)

    def project(c0):
        c1, r = _project_chunk(h, w_ref, c0)
        stage_ref[slot, :, c0:c1] = r

    chunk_starts = list(range(0, IN_PROJ_WIDTH, INPROJ_TN))
    for c0 in [c for c in chunk_starts if c < OFF_SB_G]:
        project(c0)
    later_chunks = iter([c for c in chunk_starts if c >= OFF_SB_G])
    stage_ref[slot, :, IN_PROJ_WIDTH:] = jnp.zeros((t, SLAB_WIDTH - IN_PROJ_WIDTH), BF16)

    prev_off = jnp.where(i > 0, 0.0, SB_MASKED_LOGIT).astype(F32)
    tri = tri_ref[...]
    tri_half = tri_ref[0:half, 0:half]

    def cols(h_idx, off):
        return slice(off + h_idx * SB_HEAD_DIM, off + (h_idx + 1) * SB_HEAD_DIM)

    def store_head(hd, rows, acc):
        ss = jnp.sum(acc * acc, axis=-1, keepdims=True)
        scale = sbw_ref[:, cols(hd, 0)] * stage_ref[slot, rows, cols(hd, OFF_SB_G)].astype(F32)
        o_ref[0, rows, cols(hd, 0)] = (acc * lax.rsqrt(ss + SB_HEAD_DIM * NORM_EPS) * scale).astype(BF16)

    lane_lo = lax.broadcasted_iota(jnp.int32, (half, 2 * SB_HEAD_DIM), 1) < SB_HEAD_DIM

    def blockdiag(pair):
        zero = jnp.zeros_like(pair)
        return jnp.concatenate([jnp.where(lane_lo, pair, zero), jnp.where(lane_lo, zero, pair)], axis=0)

    def pair_cols(h_idx, off):
        return slice(off + h_idx * SB_HEAD_DIM, off + (h_idx + 2) * SB_HEAD_DIM)

    tri_pair = blockdiag(jnp.concatenate([tri_half, tri_half], axis=1))
    bias_pair = jnp.concatenate([bias_ref[0:half, 0:half]] * 2, axis=1)
    logits, weights, top_logits, top_weights, top_values = {}, {}, {}, {}, {}
    for stage in range(SB_HEADS + 2):
        if stage < SB_HEADS:
            q = stage_ref[slot, :, cols(stage, OFF_SB_Q)]
            kc = cols(stage, OFF_SB_K)
            if stage % 2 == 0:
                top_logits[stage] = _sb_logits(
                    stage_ref[slot, 0:half, pair_cols(stage, OFF_SB_Q)],
                    blockdiag(stage_ref[slot, 0:half, pair_cols(stage, OFF_SB_K)]), bias_pair)
            logits[stage] = (
                _sb_logits(q[half:t], stage_ref[slot, :, kc], bias_ref[half:t, :]),
                _sb_logits(q, stage_ref[prev_slot, :, kc], None))
        if 0 <= stage - 1 < SB_HEADS:
            hd = stage - 1
            if hd % 2 == 0:
                z_pair, sp_pair = top_logits.pop(hd)
                c_pair = jnp.dot(sp_pair, tri_pair, preferred_element_type=F32)
                top_weights[hd] = (c_pair, jnp.exp2(z_pair + c_pair).astype(BF16))
            bot, prev = logits.pop(hd)
            weights[hd] = (_sb_weights(*bot, tri), _sb_weights(*prev, tri))
        if 0 <= stage - 2 < SB_HEADS:
            hd = stage - 2
            vc = cols(hd, OFF_SB_V)
            if hd % 2 == 0:
                c_pair, p_pair = top_weights.pop(hd)
                pv_pair = jnp.dot(p_pair, blockdiag(stage_ref[slot, 0:half, pair_cols(hd, OFF_SB_V)]),
                                  preferred_element_type=F32)
                for k_head in (0, 1):
                    lo = k_head * SB_HEAD_DIM
                    top_values[hd + k_head] = (c_pair[:, lo:lo + 1], pv_pair[:, lo:lo + SB_HEAD_DIM])
            ls_t, pv_t = top_values.pop(hd)
            (ls_b, p_b), (ls_p, p_p) = weights.pop(hd)
            pv_b = jnp.dot(p_b, stage_ref[slot, :, vc], preferred_element_type=F32)
            pv_p = jnp.dot(p_p, stage_ref[prev_slot, :, vc], preferred_element_type=F32)
            acc_t = pv_t + jnp.exp2(ls_t + prev_off) * pv_p[0:half]
            acc_b = pv_b + jnp.exp2(ls_b + prev_off) * pv_p[half:t]
            acc_ref[hd, 0:half] = acc_t
            acc_ref[hd, half:t] = acc_b
            store_head(hd, slice(0, half), acc_t)
            store_head(hd, slice(half, t), acc_b)
            ls_ref[hd, 0:half] = ls_t + ls_p[0:half]
            ls_ref[hd, half:t] = ls_b + ls_p[half:t]
        c0 = next(later_chunks, None)
        if c0 is not None:
            project(c0)
    for c0 in later_chunks:
        project(c0)

    more = jnp.logical_and(i >= 2, jnp.max(ls_ref[...]) > SB_UNDERFLOW_LOG2)

    @pl.when(more)
    def _():
        for hd in range(SB_HEADS):

            def tile_copies(j, hd=hd):
                rows = pl.ds(pl.multiple_of(j * t, t), t)
                return (pltpu.make_async_copy(slab_hbm.at[bi, rows, cols(hd, OFF_SB_K)], kbuf, sem.at[0]),
                        pltpu.make_async_copy(slab_hbm.at[bi, rows, cols(hd, OFF_SB_V)], vbuf, sem.at[1]))

            def cond(carry):
                j, ls, _ = carry
                return jnp.logical_and(j >= 0, jnp.max(ls) > SB_UNDERFLOW_LOG2)

            def body(carry, hd=hd, tile_copies=tile_copies):
                j, ls, acc = carry
                kc, vc = tile_copies(j)
                kc.start()
                vc.start()
                kc.wait()
                vc.wait()
                ls_j, pv = _sb_tile(stage_ref[slot, :, cols(hd, OFF_SB_Q)], kbuf[...], vbuf[...], tri, None)
                return j - 1, ls + ls_j, acc + jnp.exp2(ls) * pv

            _, _, acc = lax.while_loop(cond, body, (i - 2, ls_ref[hd], acc_ref[hd]))
            store_head(hd, slice(0, t), acc)

    slab_write(step).start()

    @pl.when(step == last_step)
    def _():
        @pl.when(step >= 1)
        def _():
            slab_write(step - 1).wait()
        slab_write(step).wait()


def _inproj_sb(x, pre_w, w_in, tri, bias, sb_norm_w):
    b, s, d = x.shape
    t = SB_TILE
    return pl.pallas_call(
        _inproj_sb_kernel,
        grid=(b, s // t),
        in_specs=[
            pl.BlockSpec((1, t, d), lambda bi, i: (bi, i, 0)),
            pl.BlockSpec((1, d), lambda bi, i: (0, 0)),
            pl.BlockSpec((d, IN_PROJ_WIDTH), lambda bi, i: (0, 0), pipeline_mode=pl.Buffered(1)),
            pl.BlockSpec((t, t), lambda bi, i: (0, 0)),
            pl.BlockSpec((t, t), lambda bi, i: (0, 0)),
            pl.BlockSpec((1, SB_WIDTH), lambda bi, i: (0, 0)),
        ],
        out_specs=[
            pl.BlockSpec(memory_space=pl.ANY),
            pl.BlockSpec((1, t, SB_WIDTH), lambda bi, i: (bi, i, 0)),
        ],
        out_shape=[
            jax.ShapeDtypeStruct((b, s, SLAB_WIDTH), BF16),
            jax.ShapeDtypeStruct((b, s, SB_WIDTH), BF16),
        ],
        scratch_shapes=[
            pltpu.VMEM((2, t, SLAB_WIDTH), BF16),
            pltpu.VMEM((SB_HEADS, t, SB_HEAD_DIM), F32),
            pltpu.VMEM((SB_HEADS, t, 1), F32),
            pltpu.VMEM((t, SB_HEAD_DIM), BF16),
            pltpu.VMEM((t, SB_HEAD_DIM), BF16),
            pltpu.SemaphoreType.DMA((2,)),
            pltpu.SemaphoreType.DMA((2,)),
        ],
        compiler_params=pltpu.CompilerParams(
            dimension_semantics=("arbitrary", "arbitrary"), vmem_limit_bytes=VMEM_LIMIT),
        name="inproj_sb",
    )(x, pre_w, w_in, tri, bias, sb_norm_w)


def _gla_gate_logits(lr, w2_ref, b2_ref):
    x = jnp.dot(lr, w2_ref[...], preferred_element_type=F32) + b2_ref[...]
    return x, jnp.min(x) > GLA_SAFE_LOG_DECAY * GATE_TEMP / GLA_CHUNK + math.log(2.0)


def _gla_decay(x, q, k, low_ref, exact):
    c = GLA_CHUNK
    kw = GLA_KEY_WIDTH
    chunks = range(x.shape[0] // c)
    g = jnp.minimum(x, 0.0) - jnp.log(1.0 + jnp.exp2(jnp.abs(x) * (-LOG2E)))
    g_hi = g.astype(BF16)
    g_lo = (g - g_hi.astype(F32)).astype(BF16)
    g_split = jnp.concatenate([g_hi, g_lo], axis=1)
    span = GLA_CUMSUM_CHUNKS * c
    bc2 = jnp.concatenate(
        [jnp.dot(low_ref[...], g_split[r0:r0 + span, :], preferred_element_type=F32)
         for r0 in range(0, g.shape[0], span)], axis=0)
    bcum = bc2[:, :kw] + bc2[:, kw:]
    q_e = jnp.exp(bcum)
    qf = q.astype(F32)
    kf = k.astype(F32)
    rs = [slice(ci * c, (ci + 1) * c) for ci in chunks]
    decay = [q_e[(ci + 1) * c - 1:(ci + 1) * c, :] for ci in chunks]
    if exact:
        k_t = None
        k_h = [(kf[rs[ci], :] * jnp.exp(bcum[(ci + 1) * c - 1:(ci + 1) * c, :] - bcum[rs[ci], :])
                ).astype(BF16) for ci in chunks]
    else:
        kk = kf * (1.0 / q_e)
        k_t = kk.astype(BF16)
        k_h = [(kk[rs[ci], :] * decay[ci]).astype(BF16) for ci in chunks]
    return dict(bcum=bcum, qf=qf, kf=kf, q_t=(qf * q_e).astype(BF16), decay=decay, k_t=k_t, k_h=k_h)


def _gla_finish(ctx, exact, v_of, state_ref, b_ref, kf_ref, store_o, fillers=()):
    c = GLA_CHUNK
    heads = range(GLA_HEADS)
    chunks = range(len(ctx["decay"]))
    ks = [slice(h * GLA_KEY_DIM, (h + 1) * GLA_KEY_DIM) for h in heads]
    rs = [slice(ci * c, (ci + 1) * c) for ci in chunks]
    q_t, k_h, decay = ctx["q_t"], ctx["k_h"], ctx["decay"]
    fillers = iter(fillers)

    def fill():
        emit = next(fillers, None)
        if emit is not None:
            emit()

    row = lax.broadcasted_iota(jnp.int32, (c, c), 0)
    col = lax.broadcasted_iota(jnp.int32, (c, c), 1)
    causal = col <= row

    if not exact:
        k_t = ctx["k_t"]
        sc = [[lax.dot_general(q_t[rs[ci], ks[h]], k_t[rs[ci], ks[h]], _NT,
                               preferred_element_type=F32) for h in heads] for ci in chunks]
        fill()
        sc = [[jnp.where(causal, s, 0.0).astype(BF16) for s in per_chunk] for per_chunk in sc]
    else:
        b_ref[...] = ctx["bcum"]
        kf_ref[...] = ctx["kf"]

        def exact_scores(ci, h):
            qh = ctx["qf"][rs[ci], ks[h]]
            bh = ctx["bcum"][rs[ci], ks[h]]

            def key_rows(grp, sc):
                base = pl.multiple_of(ci * c + grp * SUBLANES, SUBLANES)
                k_rows = kf_ref[pl.ds(base, SUBLANES), ks[h]]
                b_rows = b_ref[pl.ds(base, SUBLANES), ks[h]]
                for r in range(SUBLANES):
                    w = jnp.exp(jnp.minimum(bh - b_rows[r:r + 1, :], 0.0))
                    score_col = jnp.sum(qh * k_rows[r:r + 1, :] * w, axis=-1, keepdims=True)
                    sc = jnp.where(col == grp * SUBLANES + r, score_col, sc)
                return sc

            s = lax.fori_loop(0, c // SUBLANES, key_rows, jnp.zeros((c, c), F32))
            return jnp.where(causal, s, 0.0).astype(BF16)

        sc = [[exact_scores(ci, h) for h in heads] for ci in chunks]
        fill()

    st = [state_ref[h] for h in heads]
    o = []
    for ci in chunks:
        o.append([jnp.dot(sc[ci][h], v_of(ci, h), preferred_element_type=F32)
                  + lax.dot_general(q_t[rs[ci], ks[h]], st[h].astype(BF16), _NT,
                                    preferred_element_type=F32) for h in heads])
        st = [st[h] * decay[ci][:, ks[h]] + lax.dot_general(
            v_of(ci, h), k_h[ci][:, ks[h]], _TN, preferred_element_type=F32) for h in heads]
        fill()
    for emit in fillers:
        emit()
    for h in heads:
        state_ref[h] = st[h]
    for ci in chunks:
        for h in heads:
            store_o(ci, h, o[ci][h])


def _headnorm_gate(o_ref, g_ref, w_ref, n_heads):
    width = o_ref.shape[1] // n_heads
    parts = []
    for h in range(n_heads):
        cs = slice(h * width, (h + 1) * width)
        o = o_ref[:, cs].astype(F32)
        ss = jnp.sum(o * o, axis=-1, keepdims=True)
        scale = w_ref[:, cs] * g_ref[:, cs].astype(F32)
        parts.append((o * lax.rsqrt(ss + width * NORM_EPS) * scale).astype(BF16))
    return jnp.concatenate(parts, axis=1)


def _gla_out_kernel(q_ref, k_ref, v_ref, lr_ref, w2_ref, b2_ref, low_ref,
                    sby_ref, glag_ref, glaw_ref, wout_ref, x_ref, nw_ref,
                    o_ref, state_ref, b_ref, kf_ref, ysb_ref, glao_ref):
    c = GLA_CHUNK

    @pl.when(pl.program_id(1) == 0)
    def _():
        state_ref[...] = jnp.zeros_like(state_ref)

    x_gate, factorise = _gla_gate_logits(lr_ref[...], w2_ref, b2_ref)

    def gla_v(ci, hd):
        return v_ref[ci * c:(ci + 1) * c, hd * GLA_VAL_DIM:(hd + 1) * GLA_VAL_DIM]

    def store_gla_o(ci, hd, val):
        glao_ref[ci * c:(ci + 1) * c, hd * GLA_VAL_DIM:(hd + 1) * GLA_VAL_DIM] = val

    for exact in (False, True):
        @pl.when(factorise != exact)
        def _(exact=exact):
            gla = _gla_decay(x_gate, q_ref[...], k_ref[...], low_ref, exact)
            def sb_outproj(r0):
                def emit():
                    ysb_ref[r0:r0 + c, :] = jnp.dot(sby_ref[r0:r0 + c, :], wout_ref[0:SB_WIDTH, :],
                                                    preferred_element_type=F32)
                return emit

            _gla_finish(gla, exact, gla_v, state_ref, b_ref, kf_ref, store_gla_o,
                        fillers=[sb_outproj(r0) for r0 in range(0, GLA_CPS * c, c)])
            gla_y = _headnorm_gate(glao_ref, glag_ref, glaw_ref, GLA_HEADS)
            half = gla_y.shape[0] // 2
            for r0 in (0, half):
                rr = slice(r0, r0 + half)
                y = ysb_ref[rr, :] + jnp.dot(gla_y[rr, :], wout_ref[SB_WIDTH:, :],
                                             preferred_element_type=F32)
                ms = jnp.mean(y * y, axis=-1, keepdims=True)
                o_ref[rr, :] = x_ref[rr, :] + y * lax.rsqrt(ms + NORM_EPS) * nw_ref[...]


def _gla_out(slab, sb_y, w2_pad, b2, low, gla_norm_w, w_out, x2, post_w, seq):
    m = x2.shape[0]
    c = GLA_CHUNK * GLA_CPS
    steps = seq // c

    def rows(col_block):
        return lambda bi, i: (bi * steps + i, col_block)

    const = lambda bi, i: (0, 0)
    return pl.pallas_call(
        _gla_out_kernel,
        grid=(m // seq, steps),
        in_specs=[
            pl.BlockSpec((c, GLA_KEY_WIDTH), rows(OFF_GLA_Q // GLA_KEY_WIDTH)),
            pl.BlockSpec((c, GLA_KEY_WIDTH), rows(OFF_GLA_K // GLA_KEY_WIDTH)),
            pl.BlockSpec((c, GLA_WIDTH), rows(OFF_GLA_V // GLA_WIDTH)),
            pl.BlockSpec((c, LANES), rows(OFF_GLA_LR // LANES)),
            pl.BlockSpec((LANES, GLA_KEY_WIDTH), const),
            pl.BlockSpec((1, GLA_KEY_WIDTH), const),
            pl.BlockSpec(low.shape, const),
            pl.BlockSpec((c, SB_WIDTH), rows(0)),
            pl.BlockSpec((c, GLA_WIDTH), rows(OFF_GLA_G // GLA_WIDTH)),
            pl.BlockSpec((1, GLA_WIDTH), const),
            pl.BlockSpec((SB_WIDTH + GLA_WIDTH, D_MODEL), const),
            pl.BlockSpec((c, D_MODEL), rows(0)),
            pl.BlockSpec((1, D_MODEL), const),
        ],
        out_specs=pl.BlockSpec((c, D_MODEL), rows(0)),
        out_shape=jax.ShapeDtypeStruct((m, D_MODEL), F32),
        scratch_shapes=[
            pltpu.VMEM((GLA_HEADS, GLA_VAL_DIM, GLA_KEY_DIM), F32),
            pltpu.VMEM((c, GLA_KEY_WIDTH), F32),
            pltpu.VMEM((c, GLA_KEY_WIDTH), F32),
            pltpu.VMEM((c, D_MODEL), F32),
            pltpu.VMEM((c, GLA_WIDTH), F32),
        ],
        compiler_params=pltpu.CompilerParams(
            dimension_semantics=("arbitrary", "arbitrary"), vmem_limit_bytes=VMEM_LIMIT),
        name="gla_out",
    )(slab, slab, slab, slab, w2_pad, b2, low, sb_y, slab, gla_norm_w, w_out, x2, post_w)


def _layer(x, pre_norm_w, w_in, w_alpha2, b_alpha2, sb_norm_w, gla_norm_w, w_out, post_norm_w):
    b, s, d = x.shape
    m = b * s

    t = SB_TILE
    pos = jnp.arange(t)
    tri = -(pos[:, None] >= pos[None, :]).astype(BF16)
    bias = jnp.where(pos[None, :] < pos[:, None], 0.0, SB_MASKED_LOGIT).astype(F32)
    slab3, sb_y = _inproj_sb(x, pre_norm_w.reshape(1, d), w_in.astype(BF16), tri, bias,
                             sb_norm_w.reshape(1, SB_WIDTH) * SB_HEAD_DIM ** 0.5)

    tok = jnp.arange(GLA_CHUNK * GLA_CUMSUM_CHUNKS)
    same_chunk = (tok[:, None] // GLA_CHUNK) == (tok[None, :] // GLA_CHUNK)
    low = jnp.where(same_chunk & (tok[:, None] >= tok[None, :]), 1.0 / GATE_TEMP, 0.0).astype(BF16)
    w2_pad = jnp.pad(w_alpha2, ((0, LANES - GATE_RANK), (0, 0))).astype(BF16)
    out = _gla_out(slab3.reshape(m, SLAB_WIDTH), sb_y.reshape(m, SB_WIDTH), w2_pad,
                   b_alpha2.reshape(1, GLA_KEY_WIDTH), low,
                   gla_norm_w.reshape(1, GLA_WIDTH) * GLA_VAL_DIM ** 0.5,
                   w_out.astype(BF16), x.reshape(m, d), post_norm_w.reshape(1, d), s)
    return out.reshape(b, s, d)


def kernel(x, pre_norm_w, w_in, w_alpha2, b_alpha2, sb_norm_w, gla_norm_w, w_out, post_norm_w):
    for layer in range(pre_norm_w.shape[0]):
        x = _layer(x, pre_norm_w[layer], w_in[layer], w_alpha2[layer], b_alpha2[layer],
                   sb_norm_w[layer], gla_norm_w[layer], w_out[layer], post_norm_w[layer])
    return x
```

```python
import math

import jax
import jax.numpy as jnp
from jax import lax
from jax.experimental import pallas as pl
from jax.experimental.pallas import tpu as pltpu

F32 = jnp.float32
BF16 = jnp.bfloat16

D_MODEL = 1024
SB_HEADS = 8
SB_HEAD_DIM = 128
SB_WIDTH = SB_HEADS * SB_HEAD_DIM
GLA_HEADS = 4
GLA_KEY_DIM = 128
GLA_VAL_DIM = 256
GLA_KEY_WIDTH = GLA_HEADS * GLA_KEY_DIM
GLA_WIDTH = GLA_HEADS * GLA_VAL_DIM
GATE_RANK = 16
GATE_TEMP = 16.0
NORM_EPS = 1e-6
LANES = 128
SUBLANES = 8

OFF_SB_Q = 0
OFF_SB_K = OFF_SB_Q + SB_WIDTH
OFF_SB_V = OFF_SB_K + SB_WIDTH
OFF_SB_G = OFF_SB_V + SB_WIDTH
OFF_GLA_Q = OFF_SB_G + SB_WIDTH
OFF_GLA_K = OFF_GLA_Q + GLA_KEY_WIDTH
OFF_GLA_V = OFF_GLA_K + GLA_KEY_WIDTH
OFF_GLA_G = OFF_GLA_V + GLA_WIDTH
OFF_GLA_LR = OFF_GLA_G + GLA_WIDTH
IN_PROJ_WIDTH = OFF_GLA_LR + GATE_RANK
SLAB_WIDTH = OFF_GLA_LR + LANES

LOG2E = math.log2(math.e)

INPROJ_TN = 256
SB_TILE = 256
SB_UNDERFLOW_LOG2 = -150.0
SB_MASKED_LOGIT = -1e30
SB_SOFTPLUS_CLAMP = 64.0
GLA_CHUNK = 128
GLA_CPS = 4
GLA_CUMSUM_CHUNKS = 2
GLA_SAFE_LOG_DECAY = -60.0
V7X_VMEM_BYTES = 64 * 1024 * 1024
VMEM_LIMIT = V7X_VMEM_BYTES * 7 // 8

_NT = (((1,), (1,)), ((), ()))
_TN = (((0,), (0,)), ((), ()))


def _project_chunk(h, w_ref, c0):
    c1 = min(c0 + INPROJ_TN, IN_PROJ_WIDTH)
    r = jnp.dot(h, w_ref[:, c0:c1], preferred_element_type=F32)
    if OFF_SB_Q <= c0 < OFF_SB_K:
        r = r * (LOG2E * SB_HEAD_DIM ** -0.5)
    elif OFF_GLA_Q <= c0 < OFF_GLA_K:
        r = r * (GLA_KEY_DIM ** -0.5)
    elif OFF_SB_G <= c0 < OFF_GLA_Q or OFF_GLA_G <= c0 < OFF_GLA_LR:
        r = r * (1.0 / (1.0 + jnp.exp2(r * (-LOG2E))))
    return c1, r.astype(BF16)


def _sb_logits(q, k, bias):
    z = lax.dot_general(q, k, _NT, preferred_element_type=F32)
    if bias is not None:
        z = z + bias
    sp = jnp.maximum(z, jnp.log2(1.0 + jnp.exp2(jnp.minimum(z, SB_SOFTPLUS_CLAMP))))
    return z, sp.astype(BF16)


def _sb_weights(z, sp, tri):
    c = jnp.dot(sp, tri, preferred_element_type=F32)
    return c[:, 0:1], jnp.exp2(z + c).astype(BF16)


def _sb_tile(q, k, v, tri, bias):
    z, sp = _sb_logits(q, k, bias)
    ls, p = _sb_weights(z, sp, tri)
    return ls, jnp.dot(p, v, preferred_element_type=F32)


def _inproj_sb_kernel(x_ref, nw_ref, w_ref, tri_ref, bias_ref, sbw_ref, slab_hbm, o_ref,
                      stage_ref, acc_ref, ls_ref, kbuf, vbuf, sem, wsem):
    t = SB_TILE
    half = t // 2
    bi = pl.program_id(0)
    i = pl.program_id(1)
    nq = pl.num_programs(1)
    step = bi * nq + i
    last_step = pl.num_programs(0) * nq - 1
    slot = lax.rem(step, 2)
    prev_slot = 1 - slot

    def slab_write(step_idx):
        rows = pl.ds(pl.multiple_of(lax.rem(step_idx, nq) * t, t), t)
        s = lax.rem(step_idx, 2)
        return pltpu.make_async_copy(stage_ref.at[s], slab_hbm.at[lax.div(step_idx, nq), rows, :],
                                     wsem.at[s])

    @pl.when(step == 0)
    def _():
        stage_ref[1] = jnp.zeros(stage_ref.shape[1:], BF16)

    @pl.when(step >= 2)
    def _():
        slab_write(step - 2).wait()

    x = x_ref[0]
    ms = jnp.mean(x * x, axis=-1, keepdims=True)
    h = (x * lax.rsqrt(ms + NORM_EPS) * nw_ref[...]).astype(BF16)

    def project(c0):
        c1, r = _project_chunk(h, w_ref, c0)
        stage_ref[slot, :, c0:c1] = r

    chunk_starts = list(range(0, IN_PROJ_WIDTH, INPROJ_TN))
    for c0 in [c for c in chunk_starts if c < OFF_SB_G]:
        project(c0)
    later_chunks = iter([c for c in chunk_starts if c >= OFF_SB_G])
    stage_ref[slot, :, IN_PROJ_WIDTH:] = jnp.zeros((t, SLAB_WIDTH - IN_PROJ_WIDTH), BF16)

    prev_off = jnp.where(i > 0, 0.0, SB_MASKED_LOGIT).astype(F32)
    tri = tri_ref[...]
    tri_half = tri_ref[0:half, 0:half]

    def cols(h_idx, off):
        return slice(off + h_idx * SB_HEAD_DIM, off + (h_idx + 1) * SB_HEAD_DIM)

    def store_head(hd, rows, acc):
        ss = jnp.sum(acc * acc, axis=-1, keepdims=True)
        scale = sbw_ref[:, cols(hd, 0)] * stage_ref[slot, rows, cols(hd, OFF_SB_G)].astype(F32)
        o_ref[0, rows, cols(hd, 0)] = (acc * lax.rsqrt(ss + SB_HEAD_DIM * NORM_EPS) * scale).astype(BF16)

    lane_lo = lax.broadcasted_iota(jnp.int32, (half, 2 * SB_HEAD_DIM), 1) < SB_HEAD_DIM

    def blockdiag(pair):
        zero = jnp.zeros_like(pair)
        return jnp.concatenate([jnp.where(lane_lo, pair, zero), jnp.where(lane_lo, zero, pair)], axis=0)

    def pair_cols(h_idx, off):
        return slice(off + h_idx * SB_HEAD_DIM, off + (h_idx + 2) * SB_HEAD_DIM)

    tri_pair = blockdiag(jnp.concatenate([tri_half, tri_half], axis=1))
    bias_pair = jnp.concatenate([bias_ref[0:half, 0:half]] * 2, axis=1)
    logits, weights, top_logits, top_weights, top_values = {}, {}, {}, {}, {}
    for stage in range(SB_HEADS + 2):
        if stage < SB_HEADS:
            q = stage_ref[slot, :, cols(stage, OFF_SB_Q)]
            kc = cols(stage, OFF_SB_K)
            if stage % 2 == 0:
                top_logits[stage] = _sb_logits(
                    stage_ref[slot, 0:half, pair_cols(stage, OFF_SB_Q)],
                    blockdiag(stage_ref[slot, 0:half, pair_cols(stage, OFF_SB_K)]), bias_pair)
            logits[stage] = (
                _sb_logits(q[half:t], stage_ref[slot, :, kc], bias_ref[half:t, :]),
                _sb_logits(q, stage_ref[prev_slot, :, kc], None))
        if 0 <= stage - 1 < SB_HEADS:
            hd = stage - 1
            if hd % 2 == 0:
                z_pair, sp_pair = top_logits.pop(hd)
                c_pair = jnp.dot(sp_pair, tri_pair, preferred_element_type=F32)
                top_weights[hd] = (c_pair, jnp.exp2(z_pair + c_pair).astype(BF16))
            bot, prev = logits.pop(hd)
            weights[hd] = (_sb_weights(*bot, tri), _sb_weights(*prev, tri))
        if 0 <= stage - 2 < SB_HEADS:
            hd = stage - 2
            vc = cols(hd, OFF_SB_V)
            if hd % 2 == 0:
                c_pair, p_pair = top_weights.pop(hd)
                pv_pair = jnp.dot(p_pair, blockdiag(stage_ref[slot, 0:half, pair_cols(hd, OFF_SB_V)]),
                                  preferred_element_type=F32)
                for k_head in (0, 1):
                    lo = k_head * SB_HEAD_DIM
                    top_values[hd + k_head] = (c_pair[:, lo:lo + 1], pv_pair[:, lo:lo + SB_HEAD_DIM])
            ls_t, pv_t = top_values.pop(hd)
            (ls_b, p_b), (ls_p, p_p) = weights.pop(hd)
            pv_b = jnp.dot(p_b, stage_ref[slot, :, vc], preferred_element_type=F32)
            pv_p = jnp.dot(p_p, stage_ref[prev_slot, :, vc], preferred_element_type=F32)
            acc_t = pv_t + jnp.exp2(ls_t + prev_off) * pv_p[0:half]
            acc_b = pv_b + jnp.exp2(ls_b + prev_off) * pv_p[half:t]
            acc_ref[hd, 0:half] = acc_t
            acc_ref[hd, half:t] = acc_b
            store_head(hd, slice(0, half), acc_t)
            store_head(hd, slice(half, t), acc_b)
            ls_ref[hd, 0:half] = ls_t + ls_p[0:half]
            ls_ref[hd, half:t] = ls_b + ls_p[half:t]
        c0 = next(later_chunks, None)
        if c0 is not None:
            project(c0)
    for c0 in later_chunks:
        project(c0)

    more = jnp.logical_and(i >= 2, jnp.max(ls_ref[...]) > SB_UNDERFLOW_LOG2)

    @pl.when(more)
    def _():
        for hd in range(SB_HEADS):

            def tile_copies(j, hd=hd):
                rows = pl.ds(pl.multiple_of(j * t, t), t)
                return (pltpu.make_async_copy(slab_hbm.at[bi, rows, cols(hd, OFF_SB_K)], kbuf, sem.at[0]),
                        pltpu.make_async_copy(slab_hbm.at[bi, rows, cols(hd, OFF_SB_V)], vbuf, sem.at[1]))

            def cond(carry):
                j, ls, _ = carry
                return jnp.logical_and(j >= 0, jnp.max(ls) > SB_UNDERFLOW_LOG2)

            def body(carry, hd=hd, tile_copies=tile_copies):
                j, ls, acc = carry
                kc, vc = tile_copies(j)
                kc.start()
                vc.start()
                kc.wait()
                vc.wait()
                ls_j, pv = _sb_tile(stage_ref[slot, :, cols(hd, OFF_SB_Q)], kbuf[...], vbuf[...], tri, None)
                return j - 1, ls + ls_j, acc + jnp.exp2(ls) * pv

            _, _, acc = lax.while_loop(cond, body, (i - 2, ls_ref[hd], acc_ref[hd]))
            store_head(hd, slice(0, t), acc)

    slab_write(step).start()

    @pl.when(step == last_step)
    def _():
        @pl.when(step >= 1)
        def _():
            slab_write(step - 1).wait()
        slab_write(step).wait()


def _inproj_sb(x, pre_w, w_in, tri, bias, sb_norm_w):
    b, s, d = x.shape
    t = SB_TILE
    return pl.pallas_call(
        _inproj_sb_kernel,
        grid=(b, s // t),
        in_specs=[
            pl.BlockSpec((1, t, d), lambda bi, i: (bi, i, 0)),
            pl.BlockSpec((1, d), lambda bi, i: (0, 0)),
            pl.BlockSpec((d, IN_PROJ_WIDTH), lambda bi, i: (0, 0), pipeline_mode=pl.Buffered(1)),
            pl.BlockSpec((t, t), lambda bi, i: (0, 0)),
            pl.BlockSpec((t, t), lambda bi, i: (0, 0)),
            pl.BlockSpec((1, SB_WIDTH), lambda bi, i: (0, 0)),
        ],
        out_specs=[
            pl.BlockSpec(memory_space=pl.ANY),
            pl.BlockSpec((1, t, SB_WIDTH), lambda bi, i: (bi, i, 0)),
        ],
        out_shape=[
            jax.ShapeDtypeStruct((b, s, SLAB_WIDTH), BF16),
            jax.ShapeDtypeStruct((b, s, SB_WIDTH), BF16),
        ],
        scratch_shapes=[
            pltpu.VMEM((2, t, SLAB_WIDTH), BF16),
            pltpu.VMEM((SB_HEADS, t, SB_HEAD_DIM), F32),
            pltpu.VMEM((SB_HEADS, t, 1), F32),
            pltpu.VMEM((t, SB_HEAD_DIM), BF16),
            pltpu.VMEM((t, SB_HEAD_DIM), BF16),
            pltpu.SemaphoreType.DMA((2,)),
            pltpu.SemaphoreType.DMA((2,)),
        ],
        compiler_params=pltpu.CompilerParams(
            dimension_semantics=("arbitrary", "arbitrary"), vmem_limit_bytes=VMEM_LIMIT),
        name="inproj_sb",
    )(x, pre_w, w_in, tri, bias, sb_norm_w)


def _gla_gate_logits(lr, w2_ref, b2_ref):
    x = jnp.dot(lr, w2_ref[...], preferred_element_type=F32) + b2_ref[...]
    return x, jnp.min(x) > GLA_SAFE_LOG_DECAY * GATE_TEMP / GLA_CHUNK + math.log(2.0)


def _gla_decay(x, q, k, low_ref, exact):
    c = GLA_CHUNK
    kw = GLA_KEY_WIDTH
    chunks = range(x.shape[0] // c)
    g = jnp.minimum(x, 0.0) - jnp.log(1.0 + jnp.exp2(jnp.abs(x) * (-LOG2E)))
    g_hi = g.astype(BF16)
    g_lo = (g - g_hi.astype(F32)).astype(BF16)
    g_split = jnp.concatenate([g_hi, g_lo], axis=1)
    span = GLA_CUMSUM_CHUNKS * c
    bc2 = jnp.concatenate(
        [jnp.dot(low_ref[...], g_split[r0:r0 + span, :], preferred_element_type=F32)
         for r0 in range(0, g.shape[0], span)], axis=0)
    bcum = bc2[:, :kw] + bc2[:, kw:]
    q_e = jnp.exp(bcum)
    qf = q.astype(F32)
    kf = k.astype(F32)
    rs = [slice(ci * c, (ci + 1) * c) for ci in chunks]
    decay = [q_e[(ci + 1) * c - 1:(ci + 1) * c, :] for ci in chunks]
    if exact:
        k_t = None
        k_h = [(kf[rs[ci], :] * jnp.exp(bcum[(ci + 1) * c - 1:(ci + 1) * c, :] - bcum[rs[ci], :])
                ).astype(BF16) for ci in chunks]
    else:
        kk = kf * (1.0 / q_e)
        k_t = kk.astype(BF16)
        k_h = [(kk[rs[ci], :] * decay[ci]).astype(BF16) for ci in chunks]
    return dict(bcum=bcum, qf=qf, kf=kf, q_t=(qf * q_e).astype(BF16), decay=decay, k_t=k_t, k_h=k_h)


def _gla_finish(ctx, exact, v_of, state_ref, b_ref, kf_ref, store_o, fillers=()):
    c = GLA_CHUNK
    heads = range(GLA_HEADS)
    chunks = range(len(ctx["decay"]))
    ks = [slice(h * GLA_KEY_DIM, (h + 1) * GLA_KEY_DIM) for h in heads]
    rs = [slice(ci * c, (ci + 1) * c) for ci in chunks]
    q_t, k_h, decay = ctx["q_t"], ctx["k_h"], ctx["decay"]
    fillers = iter(fillers)

    def fill():
        emit = next(fillers, None)
        if emit is not None:
            emit()

    row = lax.broadcasted_iota(jnp.int32, (c, c), 0)
    col = lax.broadcasted_iota(jnp.int32, (c, c), 1)
    causal = col <= row

    if not exact:
        k_t = ctx["k_t"]
        sc = [[lax.dot_general(q_t[rs[ci], ks[h]], k_t[rs[ci], ks[h]], _NT,
                               preferred_element_type=F32) for h in heads] for ci in chunks]
        fill()
        sc = [[jnp.where(causal, s, 0.0).astype(BF16) for s in per_chunk] for per_chunk in sc]
    else:
        b_ref[...] = ctx["bcum"]
        kf_ref[...] = ctx["kf"]

        def exact_scores(ci, h):
            qh = ctx["qf"][rs[ci], ks[h]]
            bh = ctx["bcum"][rs[ci], ks[h]]

            def key_rows(grp, sc):
                base = pl.multiple_of(ci * c + grp * SUBLANES, SUBLANES)
                k_rows = kf_ref[pl.ds(base, SUBLANES), ks[h]]
                b_rows = b_ref[pl.ds(base, SUBLANES), ks[h]]
                for r in range(SUBLANES):
                    w = jnp.exp(jnp.minimum(bh - b_rows[r:r + 1, :], 0.0))
                    score_col = jnp.sum(qh * k_rows[r:r + 1, :] * w, axis=-1, keepdims=True)
                    sc = jnp.where(col == grp * SUBLANES + r, score_col, sc)
                return sc

            s = lax.fori_loop(0, c // SUBLANES, key_rows, jnp.zeros((c, c), F32))
            return jnp.where(causal, s, 0.0).astype(BF16)

        sc = [[exact_scores(ci, h) for h in heads] for ci in chunks]
        fill()

    def as_column(d_row):
        return jnp.sum(jnp.where(row == col, d_row, 0.0), axis=1, keepdims=True)

    st = [state_ref[h] for h in heads]
    o = []
    for ci in chunks:
        o.append([jnp.dot(jnp.concatenate([sc[ci][h], q_t[rs[ci], ks[h]]], axis=1),
                          jnp.concatenate([v_of(ci, h), st[h].astype(BF16)], axis=0),
                          preferred_element_type=F32) for h in heads])
        st = [st[h] * as_column(decay[ci][:, ks[h]]) + lax.dot_general(
            k_h[ci][:, ks[h]], v_of(ci, h), _TN, preferred_element_type=F32) for h in heads]
        fill()
    for emit in fillers:
        emit()
    for h in heads:
        state_ref[h] = st[h]
    for ci in chunks:
        for h in heads:
            store_o(ci, h, o[ci][h])


def _headnorm_gate(o_ref, g_ref, w_ref, n_heads):
    width = o_ref.shape[1] // n_heads
    parts = []
    for h in range(n_heads):
        cs = slice(h * width, (h + 1) * width)
        o = o_ref[:, cs].astype(F32)
        ss = jnp.sum(o * o, axis=-1, keepdims=True)
        scale = w_ref[:, cs] * g_ref[:, cs].astype(F32)
        parts.append((o * lax.rsqrt(ss + width * NORM_EPS) * scale).astype(BF16))
    return jnp.concatenate(parts, axis=1)


def _gla_out_kernel(q_ref, k_ref, v_ref, lr_ref, w2_ref, b2_ref, low_ref,
                    sby_ref, glag_ref, glaw_ref, wout_ref, x_ref, nw_ref,
                    o_ref, state_ref, b_ref, kf_ref, ysb_ref, glao_ref):
    c = GLA_CHUNK

    @pl.when(pl.program_id(1) == 0)
    def _():
        state_ref[...] = jnp.zeros_like(state_ref)

    x_gate, factorise = _gla_gate_logits(lr_ref[...], w2_ref, b2_ref)

    def gla_v(ci, hd):
        return v_ref[ci * c:(ci + 1) * c, hd * GLA_VAL_DIM:(hd + 1) * GLA_VAL_DIM]

    def store_gla_o(ci, hd, val):
        glao_ref[ci * c:(ci + 1) * c, hd * GLA_VAL_DIM:(hd + 1) * GLA_VAL_DIM] = val

    for exact in (False, True):
        @pl.when(factorise != exact)
        def _(exact=exact):
            gla = _gla_decay(x_gate, q_ref[...], k_ref[...], low_ref, exact)
            def sb_outproj(r0):
                def emit():
                    ysb_ref[r0:r0 + c, :] = jnp.dot(sby_ref[r0:r0 + c, :], wout_ref[0:SB_WIDTH, :],
                                                    preferred_element_type=F32)
                return emit

            _gla_finish(gla, exact, gla_v, state_ref, b_ref, kf_ref, store_gla_o,
                        fillers=[sb_outproj(r0) for r0 in range(0, GLA_CPS * c, c)])
            gla_y = _headnorm_gate(glao_ref, glag_ref, glaw_ref, GLA_HEADS)
            half = gla_y.shape[0] // 2
            for r0 in (0, half):
                rr = slice(r0, r0 + half)
                y = ysb_ref[rr, :] + jnp.dot(gla_y[rr, :], wout_ref[SB_WIDTH:, :],
                                             preferred_element_type=F32)
                ms = jnp.mean(y * y, axis=-1, keepdims=True)
                o_ref[rr, :] = x_ref[rr, :] + y * lax.rsqrt(ms + NORM_EPS) * nw_ref[...]


def _gla_out(slab, sb_y, w2_pad, b2, low, gla_norm_w, w_out, x2, post_w, seq):
    m = x2.shape[0]
    c = GLA_CHUNK * GLA_CPS
    steps = seq // c

    def rows(col_block):
        return lambda bi, i: (bi * steps + i, col_block)

    const = lambda bi, i: (0, 0)
    return pl.pallas_call(
        _gla_out_kernel,
        grid=(m // seq, steps),
        in_specs=[
            pl.BlockSpec((c, GLA_KEY_WIDTH), rows(OFF_GLA_Q // GLA_KEY_WIDTH)),
            pl.BlockSpec((c, GLA_KEY_WIDTH), rows(OFF_GLA_K // GLA_KEY_WIDTH)),
            pl.BlockSpec((c, GLA_WIDTH), rows(OFF_GLA_V // GLA_WIDTH)),
            pl.BlockSpec((c, LANES), rows(OFF_GLA_LR // LANES)),
            pl.BlockSpec((LANES, GLA_KEY_WIDTH), const),
            pl.BlockSpec((1, GLA_KEY_WIDTH), const),
            pl.BlockSpec(low.shape, const),
            pl.BlockSpec((c, SB_WIDTH), rows(0)),
            pl.BlockSpec((c, GLA_WIDTH), rows(OFF_GLA_G // GLA_WIDTH)),
            pl.BlockSpec((1, GLA_WIDTH), const),
            pl.BlockSpec((SB_WIDTH + GLA_WIDTH, D_MODEL), const),
            pl.BlockSpec((c, D_MODEL), rows(0)),
            pl.BlockSpec((1, D_MODEL), const),
        ],
        out_specs=pl.BlockSpec((c, D_MODEL), rows(0)),
        out_shape=jax.ShapeDtypeStruct((m, D_MODEL), F32),
        scratch_shapes=[
            pltpu.VMEM((GLA_HEADS, GLA_KEY_DIM, GLA_VAL_DIM), F32),
            pltpu.VMEM((c, GLA_KEY_WIDTH), F32),
            pltpu.VMEM((c, GLA_KEY_WIDTH), F32),
            pltpu.VMEM((c, D_MODEL), F32),
            pltpu.VMEM((c, GLA_WIDTH), F32),
        ],
        compiler_params=pltpu.CompilerParams(
            dimension_semantics=("arbitrary", "arbitrary"), vmem_limit_bytes=VMEM_LIMIT),
        name="gla_out",
    )(slab, slab, slab, slab, w2_pad, b2, low, sb_y, slab, gla_norm_w, w_out, x2, post_w)


def _layer(x, pre_norm_w, w_in, w_alpha2, b_alpha2, sb_norm_w, gla_norm_w, w_out, post_norm_w):
    b, s, d = x.shape
    m = b * s

    t = SB_TILE
    pos = jnp.arange(t)
    tri = -(pos[:, None] >= pos[None, :]).astype(BF16)
    bias = jnp.where(pos[None, :] < pos[:, None], 0.0, SB_MASKED_LOGIT).astype(F32)
    slab3, sb_y = _inproj_sb(x, pre_norm_w.reshape(1, d), w_in.astype(BF16), tri, bias,
                             sb_norm_w.reshape(1, SB_WIDTH) * SB_HEAD_DIM ** 0.5)

    tok = jnp.arange(GLA_CHUNK * GLA_CUMSUM_CHUNKS)
    same_chunk = (tok[:, None] // GLA_CHUNK) == (tok[None, :] // GLA_CHUNK)
    low = jnp.where(same_chunk & (tok[:, None] >= tok[None, :]), 1.0 / GATE_TEMP, 0.0).astype(BF16)
    w2_pad = jnp.pad(w_alpha2, ((0, LANES - GATE_RANK), (0, 0))).astype(BF16)
    out = _gla_out(slab3.reshape(m, SLAB_WIDTH), sb_y.reshape(m, SB_WIDTH), w2_pad,
                   b_alpha2.reshape(1, GLA_KEY_WIDTH), low,
                   gla_norm_w.reshape(1, GLA_WIDTH) * GLA_VAL_DIM ** 0.5,
                   w_out.astype(BF16), x.reshape(m, d), post_norm_w.reshape(1, d), s)
    return out.reshape(b, s, d)


def kernel(x, pre_norm_w, w_in, w_alpha2, b_alpha2, sb_norm_w, gla_norm_w, w_out, post_norm_w):
    for layer in range(pre_norm_w.shape[0]):
        x = _layer(x, pre_norm_w[layer], w_in[layer], w_alpha2[layer], b_alpha2[layer],
                   sb_norm_w[layer], gla_norm_w[layer], w_out[layer], post_norm_w[layer])
    return x
```

```python
import math

import jax
import jax.numpy as jnp
from jax import lax
from jax.experimental import pallas as pl
from jax.experimental.pallas import tpu as pltpu

F32 = jnp.float32
BF16 = jnp.bfloat16

D_MODEL = 1024
SB_HEADS = 8
SB_HEAD_DIM = 128
SB_WIDTH = SB_HEADS * SB_HEAD_DIM
GLA_HEADS = 4
GLA_KEY_DIM = 128
GLA_VAL_DIM = 256
GLA_KEY_WIDTH = GLA_HEADS * GLA_KEY_DIM
GLA_WIDTH = GLA_HEADS * GLA_VAL_DIM
GATE_RANK = 16
GATE_TEMP = 16.0
NORM_EPS = 1e-6
LANES = 128
SUBLANES = 8

OFF_SB_Q = 0
OFF_SB_K = OFF_SB_Q + SB_WIDTH
OFF_SB_V = OFF_SB_K + SB_WIDTH
OFF_SB_G = OFF_SB_V + SB_WIDTH
OFF_GLA_Q = OFF_SB_G + SB_WIDTH
OFF_GLA_K = OFF_GLA_Q + GLA_KEY_WIDTH
OFF_GLA_V = OFF_GLA_K + GLA_KEY_WIDTH
OFF_GLA_G = OFF_GLA_V + GLA_WIDTH
OFF_GLA_LR = OFF_GLA_G + GLA_WIDTH
IN_PROJ_WIDTH = OFF_GLA_LR + GATE_RANK
SLAB_WIDTH = OFF_GLA_LR + LANES

LOG2E = math.log2(math.e)

INPROJ_TN = 256
SB_TILE = 256
SB_UNDERFLOW_LOG2 = -150.0
SB_MASKED_LOGIT = -1e30
SB_SOFTPLUS_CLAMP = 64.0
GLA_CHUNK = 128
GLA_CPS = 4
GLA_CUMSUM_CHUNKS = 2
GLA_SAFE_LOG_DECAY = -60.0
V7X_VMEM_BYTES = 64 * 1024 * 1024
VMEM_LIMIT = V7X_VMEM_BYTES * 7 // 8

_NT = (((1,), (1,)), ((), ()))
_TN = (((0,), (0,)), ((), ()))


def _project_chunk(h, w_ref, c0):
    c1 = min(c0 + INPROJ_TN, IN_PROJ_WIDTH)
    r = jnp.dot(h, w_ref[:, c0:c1], preferred_element_type=F32)
    if OFF_SB_Q <= c0 < OFF_SB_K:
        r = r * (LOG2E * SB_HEAD_DIM ** -0.5)
    elif OFF_GLA_Q <= c0 < OFF_GLA_K:
        r = r * (GLA_KEY_DIM ** -0.5)
    elif OFF_SB_G <= c0 < OFF_GLA_Q or OFF_GLA_G <= c0 < OFF_GLA_LR:
        r = r * (1.0 / (1.0 + jnp.exp2(r * (-LOG2E))))
    return c1, r.astype(BF16)


def _sb_logits(q, k, bias):
    z = lax.dot_general(q, k, _NT, preferred_element_type=F32)
    if bias is not None:
        z = z + bias
    sp = jnp.maximum(z, jnp.log2(1.0 + jnp.exp2(jnp.minimum(z, SB_SOFTPLUS_CLAMP))))
    return z, sp.astype(BF16)


def _sb_weights(z, sp, tri):
    c = jnp.dot(sp, tri, preferred_element_type=F32)
    return c[:, 0:1], jnp.exp2(z + c).astype(BF16)


def _sb_tile(q, k, v, tri, bias):
    z, sp = _sb_logits(q, k, bias)
    ls, p = _sb_weights(z, sp, tri)
    return ls, jnp.dot(p, v, preferred_element_type=F32)


def _inproj_sb_kernel(x_ref, nw_ref, w_ref, tri_ref, bias_ref, sbw_ref, slab_hbm, o_ref,
                      stage_ref, acc_ref, ls_ref, kbuf, vbuf, sem, wsem):
    t = SB_TILE
    half = t // 2
    bi = pl.program_id(0)
    i = pl.program_id(1)
    nq = pl.num_programs(1)
    step = bi * nq + i
    last_step = pl.num_programs(0) * nq - 1
    slot = lax.rem(step, 2)
    prev_slot = 1 - slot

    def slab_write(step_idx):
        rows = pl.ds(pl.multiple_of(lax.rem(step_idx, nq) * t, t), t)
        s = lax.rem(step_idx, 2)
        return pltpu.make_async_copy(stage_ref.at[s], slab_hbm.at[lax.div(step_idx, nq), rows, :],
                                     wsem.at[s])

    @pl.when(step == 0)
    def _():
        stage_ref[1] = jnp.zeros(stage_ref.shape[1:], BF16)

    @pl.when(step >= 2)
    def _():
        slab_write(step - 2).wait()

    x = x_ref[0]
    ms = jnp.mean(x * x, axis=-1, keepdims=True)
    h = (x * lax.rsqrt(ms + NORM_EPS) * nw_ref[...]).astype(BF16)

    def project(c0):
        c1, r = _project_chunk(h, w_ref, c0)
        stage_ref[slot, :, c0:c1] = r

    chunk_starts = list(range(0, IN_PROJ_WIDTH, INPROJ_TN))
    for c0 in [c for c in chunk_starts if c < OFF_SB_G]:
        project(c0)
    later_chunks = iter([c for c in chunk_starts if c >= OFF_SB_G])
    stage_ref[slot, :, IN_PROJ_WIDTH:] = jnp.zeros((t, SLAB_WIDTH - IN_PROJ_WIDTH), BF16)

    prev_off = jnp.where(i > 0, 0.0, SB_MASKED_LOGIT).astype(F32)
    tri = tri_ref[...]
    tri_half = tri_ref[0:half, 0:half]

    def cols(h_idx, off):
        return slice(off + h_idx * SB_HEAD_DIM, off + (h_idx + 1) * SB_HEAD_DIM)

    def store_head(hd, rows, acc):
        ss = jnp.sum(acc * acc, axis=-1, keepdims=True)
        scale = sbw_ref[:, cols(hd, 0)] * stage_ref[slot, rows, cols(hd, OFF_SB_G)].astype(F32)
        o_ref[0, rows, cols(hd, 0)] = (acc * lax.rsqrt(ss + SB_HEAD_DIM * NORM_EPS) * scale).astype(BF16)

    lane_lo = lax.broadcasted_iota(jnp.int32, (half, 2 * SB_HEAD_DIM), 1) < SB_HEAD_DIM

    def blockdiag(pair):
        zero = jnp.zeros_like(pair)
        return jnp.concatenate([jnp.where(lane_lo, pair, zero), jnp.where(lane_lo, zero, pair)], axis=0)

    def pair_cols(h_idx, off):
        return slice(off + h_idx * SB_HEAD_DIM, off + (h_idx + 2) * SB_HEAD_DIM)

    tri_pair = blockdiag(jnp.concatenate([tri_half, tri_half], axis=1))
    bias_pair = jnp.concatenate([bias_ref[0:half, 0:half]] * 2, axis=1)
    logits, weights, top_logits, top_weights, top_values = {}, {}, {}, {}, {}
    for stage in range(SB_HEADS + 2):
        if stage < SB_HEADS:
            q = stage_ref[slot, :, cols(stage, OFF_SB_Q)]
            kc = cols(stage, OFF_SB_K)
            if stage % 2 == 0:
                top_logits[stage] = _sb_logits(
                    stage_ref[slot, 0:half, pair_cols(stage, OFF_SB_Q)],
                    blockdiag(stage_ref[slot, 0:half, pair_cols(stage, OFF_SB_K)]), bias_pair)
            logits[stage] = (
                _sb_logits(q[half:t], stage_ref[slot, :, kc], bias_ref[half:t, :]),
                _sb_logits(q, stage_ref[prev_slot, :, kc], None))
        if 0 <= stage - 1 < SB_HEADS:
            hd = stage - 1
            if hd % 2 == 0:
                z_pair, sp_pair = top_logits.pop(hd)
                c_pair = jnp.dot(sp_pair, tri_pair, preferred_element_type=F32)
                top_weights[hd] = (c_pair, jnp.exp2(z_pair + c_pair).astype(BF16))
            bot, prev = logits.pop(hd)
            weights[hd] = (_sb_weights(*bot, tri), _sb_weights(*prev, tri))
        if 0 <= stage - 2 < SB_HEADS:
            hd = stage - 2
            vc = cols(hd, OFF_SB_V)
            if hd % 2 == 0:
                c_pair, p_pair = top_weights.pop(hd)
                pv_pair = jnp.dot(p_pair, blockdiag(stage_ref[slot, 0:half, pair_cols(hd, OFF_SB_V)]),
                                  preferred_element_type=F32)
                for k_head in (0, 1):
                    lo = k_head * SB_HEAD_DIM
                    top_values[hd + k_head] = (c_pair[:, lo:lo + 1], pv_pair[:, lo:lo + SB_HEAD_DIM])
            ls_t, pv_t = top_values.pop(hd)
            (ls_b, p_b), (ls_p, p_p) = weights.pop(hd)
            pv_b = jnp.dot(p_b, stage_ref[slot, :, vc], preferred_element_type=F32)
            pv_p = jnp.dot(p_p, stage_ref[prev_slot, :, vc], preferred_element_type=F32)
            acc_t = pv_t + jnp.exp2(ls_t + prev_off) * pv_p[0:half]
            acc_b = pv_b + jnp.exp2(ls_b + prev_off) * pv_p[half:t]
            acc_ref[hd, 0:half] = acc_t
            acc_ref[hd, half:t] = acc_b
            store_head(hd, slice(0, half), acc_t)
            store_head(hd, slice(half, t), acc_b)
            ls_ref[hd, 0:half] = ls_t + ls_p[0:half]
            ls_ref[hd, half:t] = ls_b + ls_p[half:t]
        c0 = next(later_chunks, None)
        if c0 is not None:
            project(c0)
    for c0 in later_chunks:
        project(c0)

    more = jnp.logical_and(i >= 2, jnp.max(ls_ref[...]) > SB_UNDERFLOW_LOG2)

    @pl.when(more)
    def _():
        for hd in range(SB_HEADS):

            def tile_copies(j, hd=hd):
                rows = pl.ds(pl.multiple_of(j * t, t), t)
                return (pltpu.make_async_copy(slab_hbm.at[bi, rows, cols(hd, OFF_SB_K)], kbuf, sem.at[0]),
                        pltpu.make_async_copy(slab_hbm.at[bi, rows, cols(hd, OFF_SB_V)], vbuf, sem.at[1]))

            def cond(carry):
                j, ls, _ = carry
                return jnp.logical_and(j >= 0, jnp.max(ls) > SB_UNDERFLOW_LOG2)

            def body(carry, hd=hd, tile_copies=tile_copies):
                j, ls, acc = carry
                kc, vc = tile_copies(j)
                kc.start()
                vc.start()
                kc.wait()
                vc.wait()
                ls_j, pv = _sb_tile(stage_ref[slot, :, cols(hd, OFF_SB_Q)], kbuf[...], vbuf[...], tri, None)
                return j - 1, ls + ls_j, acc + jnp.exp2(ls) * pv

            _, _, acc = lax.while_loop(cond, body, (i - 2, ls_ref[hd], acc_ref[hd]))
            store_head(hd, slice(0, t), acc)

    slab_write(step).start()

    @pl.when(step == last_step)
    def _():
        @pl.when(step >= 1)
        def _():
            slab_write(step - 1).wait()
        slab_write(step).wait()


def _inproj_sb(x, pre_w, w_in, tri, bias, sb_norm_w):
    b, s, d = x.shape
    t = SB_TILE
    return pl.pallas_call(
        _inproj_sb_kernel,
        grid=(b, s // t),
        in_specs=[
            pl.BlockSpec((1, t, d), lambda bi, i: (bi, i, 0)),
            pl.BlockSpec((1, d), lambda bi, i: (0, 0)),
            pl.BlockSpec((d, IN_PROJ_WIDTH), lambda bi, i: (0, 0), pipeline_mode=pl.Buffered(1)),
            pl.BlockSpec((t, t), lambda bi, i: (0, 0)),
            pl.BlockSpec((t, t), lambda bi, i: (0, 0)),
            pl.BlockSpec((1, SB_WIDTH), lambda bi, i: (0, 0)),
        ],
        out_specs=[
            pl.BlockSpec(memory_space=pl.ANY),
            pl.BlockSpec((1, t, SB_WIDTH), lambda bi, i: (bi, i, 0)),
        ],
        out_shape=[
            jax.ShapeDtypeStruct((b, s, SLAB_WIDTH), BF16),
            jax.ShapeDtypeStruct((b, s, SB_WIDTH), BF16),
        ],
        scratch_shapes=[
            pltpu.VMEM((2, t, SLAB_WIDTH), BF16),
            pltpu.VMEM((SB_HEADS, t, SB_HEAD_DIM), F32),
            pltpu.VMEM((SB_HEADS, t, 1), F32),
            pltpu.VMEM((t, SB_HEAD_DIM), BF16),
            pltpu.VMEM((t, SB_HEAD_DIM), BF16),
            pltpu.SemaphoreType.DMA((2,)),
            pltpu.SemaphoreType.DMA((2,)),
        ],
        compiler_params=pltpu.CompilerParams(
            dimension_semantics=("arbitrary", "arbitrary"), vmem_limit_bytes=VMEM_LIMIT),
        name="inproj_sb",
    )(x, pre_w, w_in, tri, bias, sb_norm_w)


def _gla_gate_logits(lr, w2_ref, b2_ref):
    x = jnp.dot(lr, w2_ref[...], preferred_element_type=F32) + b2_ref[...]
    return x, jnp.min(x) > GLA_SAFE_LOG_DECAY * GATE_TEMP / GLA_CHUNK + math.log(2.0)


def _gla_decay(x, q, k, low_ref, exact):
    c = GLA_CHUNK
    kw = GLA_KEY_WIDTH
    chunks = range(x.shape[0] // c)
    g = jnp.minimum(x, 0.0) - jnp.log(1.0 + jnp.exp2(jnp.abs(x) * (-LOG2E)))
    g_hi = g.astype(BF16)
    g_lo = (g - g_hi.astype(F32)).astype(BF16)
    g_split = jnp.concatenate([g_hi, g_lo], axis=1)
    span = GLA_CUMSUM_CHUNKS * c
    bc2 = jnp.concatenate(
        [jnp.dot(low_ref[...], g_split[r0:r0 + span, :], preferred_element_type=F32)
         for r0 in range(0, g.shape[0], span)], axis=0)
    bcum = bc2[:, :kw] + bc2[:, kw:]
    q_e = jnp.exp(bcum)
    qf = q.astype(F32)
    kf = k.astype(F32)
    rs = [slice(ci * c, (ci + 1) * c) for ci in chunks]
    decay = [q_e[(ci + 1) * c - 1:(ci + 1) * c, :] for ci in chunks]
    if exact:
        k_t = None
        k_h = [(kf[rs[ci], :] * jnp.exp(bcum[(ci + 1) * c - 1:(ci + 1) * c, :] - bcum[rs[ci], :])
                ).astype(BF16) for ci in chunks]
    else:
        kk = kf * (1.0 / q_e)
        k_t = kk.astype(BF16)
        k_h = [(kk[rs[ci], :] * decay[ci]).astype(BF16) for ci in chunks]
    return dict(bcum=bcum, qf=qf, kf=kf, q_t=(qf * q_e).astype(BF16), decay=decay, k_t=k_t, k_h=k_h)


def _gla_finish(ctx, exact, v_of, state_ref, b_ref, kf_ref, store_o, fillers=()):
    c = GLA_CHUNK
    heads = range(GLA_HEADS)
    chunks = range(len(ctx["decay"]))
    ks = [slice(h * GLA_KEY_DIM, (h + 1) * GLA_KEY_DIM) for h in heads]
    rs = [slice(ci * c, (ci + 1) * c) for ci in chunks]
    q_t, k_h, decay = ctx["q_t"], ctx["k_h"], ctx["decay"]
    fillers = iter(fillers)

    def fill():
        emit = next(fillers, None)
        if emit is not None:
            emit()

    row = lax.broadcasted_iota(jnp.int32, (c, c), 0)
    col = lax.broadcasted_iota(jnp.int32, (c, c), 1)
    causal = col <= row

    if not exact:
        k_t = ctx["k_t"]
        lane_lo = lax.broadcasted_iota(jnp.int32, (c, 2 * GLA_KEY_DIM), 1) < GLA_KEY_DIM

        def blockdiag(pair):
            zero = jnp.zeros_like(pair)
            return jnp.concatenate([jnp.where(lane_lo, pair, zero), jnp.where(lane_lo, zero, pair)], axis=0)

        kp = [slice(h * GLA_KEY_DIM, (h + 2) * GLA_KEY_DIM) for h in range(0, GLA_HEADS, 2)]
        sc = [[lax.dot_general(q_t[rs[ci], p], blockdiag(k_t[rs[ci], p]), _NT,
                               preferred_element_type=F32) for p in kp] for ci in chunks]
        fill()
        sc = [[jnp.where(causal, sc[ci][h // 2][:, (h % 2) * c:(h % 2 + 1) * c], 0.0).astype(BF16)
               for h in heads] for ci in chunks]
    else:
        b_ref[...] = ctx["bcum"]
        kf_ref[...] = ctx["kf"]

        def exact_scores(ci, h):
            qh = ctx["qf"][rs[ci], ks[h]]
            bh = ctx["bcum"][rs[ci], ks[h]]

            def key_rows(grp, sc):
                base = pl.multiple_of(ci * c + grp * SUBLANES, SUBLANES)
                k_rows = kf_ref[pl.ds(base, SUBLANES), ks[h]]
                b_rows = b_ref[pl.ds(base, SUBLANES), ks[h]]
                for r in range(SUBLANES):
                    w = jnp.exp(jnp.minimum(bh - b_rows[r:r + 1, :], 0.0))
                    score_col = jnp.sum(qh * k_rows[r:r + 1, :] * w, axis=-1, keepdims=True)
                    sc = jnp.where(col == grp * SUBLANES + r, score_col, sc)
                return sc

            s = lax.fori_loop(0, c // SUBLANES, key_rows, jnp.zeros((c, c), F32))
            return jnp.where(causal, s, 0.0).astype(BF16)

        sc = [[exact_scores(ci, h) for h in heads] for ci in chunks]
        fill()

    def as_column(d_row):
        return jnp.sum(jnp.where(row == col, d_row, 0.0), axis=1, keepdims=True)

    st = [state_ref[h] for h in heads]
    o = []
    for ci in chunks:
        o.append([jnp.dot(jnp.concatenate([sc[ci][h], q_t[rs[ci], ks[h]]], axis=1),
                          jnp.concatenate([v_of(ci, h), st[h].astype(BF16)], axis=0),
                          preferred_element_type=F32) for h in heads])
        st = [st[h] * as_column(decay[ci][:, ks[h]]) + lax.dot_general(
            k_h[ci][:, ks[h]], v_of(ci, h), _TN, preferred_element_type=F32) for h in heads]
        fill()
    for emit in fillers:
        emit()
    for h in heads:
        state_ref[h] = st[h]
    for ci in chunks:
        for h in heads:
            store_o(ci, h, o[ci][h])


def _headnorm_gate(o_ref, g_ref, w_ref, n_heads):
    width = o_ref.shape[1] // n_heads
    parts = []
    for h in range(n_heads):
        cs = slice(h * width, (h + 1) * width)
        o = o_ref[:, cs].astype(F32)
        ss = jnp.sum(o * o, axis=-1, keepdims=True)
        scale = w_ref[:, cs] * g_ref[:, cs].astype(F32)
        parts.append((o * lax.rsqrt(ss + width * NORM_EPS) * scale).astype(BF16))
    return jnp.concatenate(parts, axis=1)


def _gla_out_kernel(q_ref, k_ref, v_ref, lr_ref, w2_ref, b2_ref, low_ref,
                    sby_ref, glag_ref, glaw_ref, wout_ref, x_ref, nw_ref,
                    o_ref, state_ref, b_ref, kf_ref, ysb_ref, glao_ref):
    c = GLA_CHUNK

    @pl.when(pl.program_id(1) == 0)
    def _():
        state_ref[...] = jnp.zeros_like(state_ref)

    x_gate, factorise = _gla_gate_logits(lr_ref[...], w2_ref, b2_ref)

    def gla_v(ci, hd):
        return v_ref[ci * c:(ci + 1) * c, hd * GLA_VAL_DIM:(hd + 1) * GLA_VAL_DIM]

    def store_gla_o(ci, hd, val):
        glao_ref[ci * c:(ci + 1) * c, hd * GLA_VAL_DIM:(hd + 1) * GLA_VAL_DIM] = val

    for exact in (False, True):
        @pl.when(factorise != exact)
        def _(exact=exact):
            gla = _gla_decay(x_gate, q_ref[...], k_ref[...], low_ref, exact)
            def sb_outproj(r0):
                def emit():
                    ysb_ref[r0:r0 + c, :] = jnp.dot(sby_ref[r0:r0 + c, :], wout_ref[0:SB_WIDTH, :],
                                                    preferred_element_type=F32)
                return emit

            _gla_finish(gla, exact, gla_v, state_ref, b_ref, kf_ref, store_gla_o,
                        fillers=[sb_outproj(r0) for r0 in range(0, GLA_CPS * c, c)])
            gla_y = _headnorm_gate(glao_ref, glag_ref, glaw_ref, GLA_HEADS)
            half = gla_y.shape[0] // 2
            for r0 in (0, half):
                rr = slice(r0, r0 + half)
                y = ysb_ref[rr, :] + jnp.dot(gla_y[rr, :], wout_ref[SB_WIDTH:, :],
                                             preferred_element_type=F32)
                ms = jnp.mean(y * y, axis=-1, keepdims=True)
                o_ref[rr, :] = x_ref[rr, :] + y * lax.rsqrt(ms + NORM_EPS) * nw_ref[...]


def _gla_out(slab, sb_y, w2_pad, b2, low, gla_norm_w, w_out, x2, post_w, seq):
    m = x2.shape[0]
    c = GLA_CHUNK * GLA_CPS
    steps = seq // c

    def rows(col_block):
        return lambda bi, i: (bi * steps + i, col_block)

    const = lambda bi, i: (0, 0)
    return pl.pallas_call(
        _gla_out_kernel,
        grid=(m // seq, steps),
        in_specs=[
            pl.BlockSpec((c, GLA_KEY_WIDTH), rows(OFF_GLA_Q // GLA_KEY_WIDTH)),
            pl.BlockSpec((c, GLA_KEY_WIDTH), rows(OFF_GLA_K // GLA_KEY_WIDTH)),
            pl.BlockSpec((c, GLA_WIDTH), rows(OFF_GLA_V // GLA_WIDTH)),
            pl.BlockSpec((c, LANES), rows(OFF_GLA_LR // LANES)),
            pl.BlockSpec((LANES, GLA_KEY_WIDTH), const),
            pl.BlockSpec((1, GLA_KEY_WIDTH), const),
            pl.BlockSpec(low.shape, const),
            pl.BlockSpec((c, SB_WIDTH), rows(0)),
            pl.BlockSpec((c, GLA_WIDTH), rows(OFF_GLA_G // GLA_WIDTH)),
            pl.BlockSpec((1, GLA_WIDTH), const),
            pl.BlockSpec((SB_WIDTH + GLA_WIDTH, D_MODEL), const),
            pl.BlockSpec((c, D_MODEL), rows(0)),
            pl.BlockSpec((1, D_MODEL), const),
        ],
        out_specs=pl.BlockSpec((c, D_MODEL), rows(0)),
        out_shape=jax.ShapeDtypeStruct((m, D_MODEL), F32),
        scratch_shapes=[
            pltpu.VMEM((GLA_HEADS, GLA_KEY_DIM, GLA_VAL_DIM), F32),
            pltpu.VMEM((c, GLA_KEY_WIDTH), F32),
            pltpu.VMEM((c, GLA_KEY_WIDTH), F32),
            pltpu.VMEM((c, D_MODEL), F32),
            pltpu.VMEM((c, GLA_WIDTH), F32),
        ],
        compiler_params=pltpu.CompilerParams(
            dimension_semantics=("arbitrary", "arbitrary"), vmem_limit_bytes=VMEM_LIMIT),
        name="gla_out",
    )(slab, slab, slab, slab, w2_pad, b2, low, sb_y, slab, gla_norm_w, w_out, x2, post_w)


def _layer(x, pre_norm_w, w_in, w_alpha2, b_alpha2, sb_norm_w, gla_norm_w, w_out, post_norm_w):
    b, s, d = x.shape
    m = b * s

    t = SB_TILE
    pos = jnp.arange(t)
    tri = -(pos[:, None] >= pos[None, :]).astype(BF16)
    bias = jnp.where(pos[None, :] < pos[:, None], 0.0, SB_MASKED_LOGIT).astype(F32)
    slab3, sb_y = _inproj_sb(x, pre_norm_w.reshape(1, d), w_in.astype(BF16), tri, bias,
                             sb_norm_w.reshape(1, SB_WIDTH) * SB_HEAD_DIM ** 0.5)

    tok = jnp.arange(GLA_CHUNK * GLA_CUMSUM_CHUNKS)
    same_chunk = (tok[:, None] // GLA_CHUNK) == (tok[None, :] // GLA_CHUNK)
    low = jnp.where(same_chunk & (tok[:, None] >= tok[None, :]), 1.0 / GATE_TEMP, 0.0).astype(BF16)
    w2_pad = jnp.pad(w_alpha2, ((0, LANES - GATE_RANK), (0, 0))).astype(BF16)
    out = _gla_out(slab3.reshape(m, SLAB_WIDTH), sb_y.reshape(m, SB_WIDTH), w2_pad,
                   b_alpha2.reshape(1, GLA_KEY_WIDTH), low,
                   gla_norm_w.reshape(1, GLA_WIDTH) * GLA_VAL_DIM ** 0.5,
                   w_out.astype(BF16), x.reshape(m, d), post_norm_w.reshape(1, d), s)
    return out.reshape(b, s, d)


def kernel(x, pre_norm_w, w_in, w_alpha2, b_alpha2, sb_norm_w, gla_norm_w, w_out, post_norm_w):
    for layer in range(pre_norm_w.shape[0]):
        x = _layer(x, pre_norm_w[layer], w_in[layer], w_alpha2[layer], b_alpha2[layer],
                   sb_norm_w[layer], gla_norm_w[layer], w_out[layer], post_norm_w[layer])
    return x
```
